```python
import math
import jax, jax.numpy as jnp
from jax import lax
import numpy as np

D_MODEL = 1024
BATCH = 2
SEQ = 8192
DEPTH = 4

N_A_LAYERS = DEPTH // 2
N_B_LAYERS = DEPTH - N_A_LAYERS
HEAD_DIM = 128
GDN_HEADS = 6
GDN_WIDTH = GDN_HEADS * HEAD_DIM
CONV_WIDTH = 4
CHUNK = 64
FOX_HEADS = 6
FOX_WIDTH = FOX_HEADS * HEAD_DIM
Q_BLOCK = 128
MEM_LEN = 256
MEM_HEADS = 4
MEM_HEAD_DIM = 64
MEM_WIDTH = MEM_HEADS * MEM_HEAD_DIM
MIX_WIDTH = GDN_WIDTH + MEM_WIDTH
A_IN_WIDTH = 4 * GDN_WIDTH + 2 * GDN_HEADS + MEM_WIDTH
B_IN_WIDTH = FOX_WIDTH + MEM_WIDTH
KV_WIDTH = 2 * FOX_WIDTH + FOX_HEADS
FFN_HIDDEN = 2816
EPS = 1e-6
NEG_INF = -1e30

kernel_name = "yoco_gdn_fox_macaron_memory"


def rmsnorm(x, gain):
    x32 = x.astype(jnp.float32)
    y = x32 * lax.rsqrt(jnp.mean(x32 * x32, axis=-1, keepdims=True) + EPS)
    return (y * gain.astype(jnp.float32)).astype(x.dtype)


def l2norm(x):
    x32 = x.astype(jnp.float32)
    return x32 * lax.rsqrt(jnp.sum(x32 * x32, axis=-1, keepdims=True) + EPS)


def swiglu(x, w_gate_up, w_down):
    gate, up = jnp.split(x @ w_gate_up, 2, axis=-1)
    return (jax.nn.silu(gate) * up) @ w_down


def causal_depthwise_conv(x, w):
    c = x.shape[-1]
    return lax.conv_general_dilated(
        x, w[:, None, :].astype(x.dtype), window_strides=(1,),
        padding=((CONV_WIDTH - 1, 0),), dimension_numbers=("NWC", "WIO", "NWC"),
        feature_group_count=c)


def chunk_gated_delta_rule(q, k, v, g, beta):
    bsz, s, h, dk = q.shape
    dv = v.shape[-1]
    n = s // CHUNK

    def chunks(t):
        return t.reshape(bsz, n, CHUNK, h, -1).transpose(0, 3, 1, 2, 4)

    q, k, v = chunks(q), chunks(k), chunks(v)
    g = g.reshape(bsz, n, CHUNK, h).transpose(0, 3, 1, 2)
    beta = beta.reshape(bsz, n, CHUNK, h).transpose(0, 3, 1, 2)
    g = jnp.cumsum(g, axis=-1)
    causal = jnp.tril(jnp.ones((CHUNK, CHUNK), dtype=bool))
    strict = jnp.tril(jnp.ones((CHUNK, CHUNK), dtype=bool), k=-1)
    gdiff = g[..., :, None] - g[..., None, :]
    decay = jnp.where(causal, jnp.exp(jnp.where(causal, gdiff, 0.0)), 0.0)
    k_beta = k * beta[..., None]
    lower = jnp.where(strict, jnp.einsum("bhnid,bhnjd->bhnij", k_beta, k) * decay, 0.0)
    system = lower + jnp.eye(CHUNK, dtype=jnp.float32)
    rhs = jnp.concatenate([v * beta[..., None], k_beta * jnp.exp(g)[..., None]], axis=-1)
    sol = lax.linalg.triangular_solve(system, rhs, left_side=True, lower=True,
                                      unit_diagonal=True)
    u_vals, w_keys = sol[..., :dv], sol[..., dv:]
    qk = jnp.where(causal, jnp.einsum("bhnid,bhnjd->bhnij", q, k) * decay, 0.0)
    g_last = g[..., -1]
    k_tail = k * jnp.exp(g_last[..., None] - g)[..., None]
    q_head = q * jnp.exp(g)[..., None]
    xs = tuple(jnp.moveaxis(t, 2, 0) for t in (q_head, qk, u_vals, w_keys, k_tail, g_last))

    def step(state, inp):
        q_c, qk_c, u_c, w_c, kt_c, gl_c = inp
        v_new = u_c - jnp.einsum("bhck,bhkv->bhcv", w_c, state)
        out = (jnp.einsum("bhck,bhkv->bhcv", q_c, state)
               + jnp.einsum("bhij,bhjv->bhiv", qk_c, v_new))
        state = (state * jnp.exp(gl_c)[..., None, None]
                 + jnp.einsum("bhck,bhcv->bhkv", kt_c, v_new))
        return state, out

    state0 = jnp.zeros((bsz, h, dk, dv), jnp.float32)
    _, out = lax.scan(step, state0, xs)
    return out.transpose(1, 0, 3, 2, 4).reshape(bsz, s, h, dv)


def gdn_mixer(u, w_in, conv_w, A_log, dt_bias, out_norm):
    bsz, s, _ = u.shape
    proj = u @ w_in
    o0 = 3 * GDN_WIDTH
    o1 = 4 * GDN_WIDTH
    qkv = proj[..., :o0]
    z = proj[..., o0:o1]
    a = proj[..., o1:o1 + GDN_HEADS]
    b = proj[..., o1 + GDN_HEADS:o1 + 2 * GDN_HEADS]
    q_mem = proj[..., o1 + 2 * GDN_HEADS:]
    qkv = jax.nn.silu(causal_depthwise_conv(qkv, conv_w))
    qkv = qkv.reshape(bsz, s, 3, GDN_HEADS, HEAD_DIM)
    q = l2norm(qkv[:, :, 0]) * (HEAD_DIM ** -0.5)
    k = l2norm(qkv[:, :, 1])
    v = qkv[:, :, 2].astype(jnp.float32)
    beta = jax.nn.sigmoid(b.astype(jnp.float32))
    g = -jnp.exp(A_log.astype(jnp.float32)) * jax.nn.softplus(
        a.astype(jnp.float32) + dt_bias.astype(jnp.float32))
    o = chunk_gated_delta_rule(q, k, v, g, beta)
    z = z.reshape(bsz, s, GDN_HEADS, HEAD_DIM).astype(jnp.float32)
    o = rmsnorm(o, out_norm) * jax.nn.silu(z)
    return o.reshape(bsz, s, GDN_WIDTH).astype(u.dtype), q_mem


def shared_fox_kv(h, kv_norm, kv_w, kv_b_f):
    bsz, s, _ = h.shape
    p = rmsnorm(h, kv_norm) @ kv_w
    k = p[..., :FOX_WIDTH].reshape(bsz, s, FOX_HEADS, HEAD_DIM)
    v = p[..., FOX_WIDTH:2 * FOX_WIDTH].reshape(bsz, s, FOX_HEADS, HEAD_DIM)
    log_f = jax.nn.log_sigmoid(p[..., 2 * FOX_WIDTH:].astype(jnp.float32)
                               + kv_b_f.astype(jnp.float32))
    c = jnp.cumsum(log_f, axis=1).transpose(0, 2, 1)
    return k, v, c


def forgetting_attention(q, k, v, c):
    bsz, s, h, d = q.shape
    nb = s // Q_BLOCK
    scale = d ** -0.5
    q_blocks = q.reshape(bsz, nb, Q_BLOCK, h, d).transpose(1, 0, 3, 2, 4)
    c_blocks = c.reshape(bsz, h, nb, Q_BLOCK).transpose(2, 0, 1, 3)
    key_pos = jnp.arange(s)

    def attend(args):
        q_i, c_i, i = args
        logits = jnp.einsum("bhqd,bshd->bhqs", q_i, k).astype(jnp.float32) * scale
        logits = logits + c_i[..., None] - c[:, :, None, :]
        q_pos = i * Q_BLOCK + jnp.arange(Q_BLOCK)
        causal = key_pos[None, :] <= q_pos[:, None]
        p = jax.nn.softmax(jnp.where(causal, logits, NEG_INF), axis=-1)
        return jnp.einsum("bhqs,bshd->bqhd", p.astype(v.dtype), v)

    out = lax.map(attend, (q_blocks, c_blocks, jnp.arange(nb)))
    return out.transpose(1, 0, 2, 3, 4).reshape(bsz, s, h * d).astype(q.dtype)


def memory_attention(q, mem_n, w_kv):
    bsz, s, _ = q.shape
    m = mem_n.shape[1]
    kv = mem_n @ w_kv
    k = kv[..., :MEM_WIDTH].reshape(bsz, m, MEM_HEADS, MEM_HEAD_DIM)
    v = kv[..., MEM_WIDTH:].reshape(bsz, m, MEM_HEADS, MEM_HEAD_DIM)
    q = q.reshape(bsz, s, MEM_HEADS, MEM_HEAD_DIM)
    logits = jnp.einsum("bqhd,bmhd->bhqm", q, k).astype(jnp.float32) * (MEM_HEAD_DIM ** -0.5)
    p = jax.nn.softmax(logits, axis=-1)
    out = jnp.einsum("bhqm,bmhd->bqhd", p.astype(v.dtype), v)
    return out.reshape(bsz, s, MEM_WIDTH)


def setup_inputs(seed: int = 0) -> dict:
    key = jax.random.key(seed)
    ks = iter(jax.random.split(key, 32))

    def nrm(shape, scale):
        return scale * jax.random.normal(next(ks), shape, jnp.float32)

    def gain(shape):
        return 1.0 + nrm(shape, 0.02)

    x = nrm((BATCH, SEQ, D_MODEL), 1.0)
    mem = nrm((BATCH, MEM_LEN, D_MODEL), 1.0)
    ffn1_norm = gain((DEPTH, D_MODEL))
    ffn1_w_gate_up = nrm((DEPTH, D_MODEL, 2 * FFN_HIDDEN), D_MODEL ** -0.5)
    ffn1_w_down = nrm((DEPTH, FFN_HIDDEN, D_MODEL), FFN_HIDDEN ** -0.5)
    mix_norm = gain((DEPTH, D_MODEL))
    ffn2_norm = gain((DEPTH, D_MODEL))
    ffn2_w_gate_up = nrm((DEPTH, D_MODEL, 2 * FFN_HIDDEN), D_MODEL ** -0.5)
    ffn2_w_down = nrm((DEPTH, FFN_HIDDEN, D_MODEL), FFN_HIDDEN ** -0.5)
    gdn_w_in = nrm((N_A_LAYERS, D_MODEL, A_IN_WIDTH), D_MODEL ** -0.5)
    gdn_conv = nrm((N_A_LAYERS, CONV_WIDTH, 3 * GDN_WIDTH), CONV_WIDTH ** -0.5)
    gdn_A_log = jnp.log(jax.random.uniform(next(ks), (N_A_LAYERS, GDN_HEADS),
                                           jnp.float32, 1.0, 16.0))
    dt = jnp.exp(jax.random.uniform(next(ks), (N_A_LAYERS, GDN_HEADS), jnp.float32,
                                    math.log(1e-3), math.log(1e-1)))
    gdn_dt_bias = dt + jnp.log(-jnp.expm1(-dt))
    gdn_out_norm = gain((N_A_LAYERS, HEAD_DIM))
    fox_w_in = nrm((N_B_LAYERS, D_MODEL, B_IN_WIDTH), D_MODEL ** -0.5)
    w_out = nrm((DEPTH, MIX_WIDTH, D_MODEL), MIX_WIDTH ** -0.5)
    mem_norm = gain((D_MODEL,))
    mem_w_kv = nrm((DEPTH, D_MODEL, 2 * MEM_WIDTH), D_MODEL ** -0.5)
    kv_norm = gain((D_MODEL,))
    kv_w = nrm((D_MODEL, KV_WIDTH), D_MODEL ** -0.5)
    kv_b_f = 2.0 + nrm((FOX_HEADS,), 0.1)
    final_norm = gain((D_MODEL,))
    return {"x": x, "mem": mem,
            "ffn1_norm": ffn1_norm, "ffn1_w_gate_up": ffn1_w_gate_up, "ffn1_w_down": ffn1_w_down,
            "mix_norm": mix_norm,
            "ffn2_norm": ffn2_norm, "ffn2_w_gate_up": ffn2_w_gate_up, "ffn2_w_down": ffn2_w_down,
            "gdn_w_in": gdn_w_in, "gdn_conv": gdn_conv, "gdn_A_log": gdn_A_log,
            "gdn_dt_bias": gdn_dt_bias, "gdn_out_norm": gdn_out_norm,
            "fox_w_in": fox_w_in, "w_out": w_out,
            "mem_norm": mem_norm, "mem_w_kv": mem_w_kv,
            "kv_norm": kv_norm, "kv_w": kv_w, "kv_b_f": kv_b_f,
            "final_norm": final_norm}


def reference(x, mem, ffn1_norm, ffn1_w_gate_up, ffn1_w_down, mix_norm,
              ffn2_norm, ffn2_w_gate_up, ffn2_w_down,
              gdn_w_in, gdn_conv, gdn_A_log, gdn_dt_bias, gdn_out_norm,
              fox_w_in, w_out, mem_norm, mem_w_kv, kv_norm, kv_w, kv_b_f, final_norm):
    bsz, s, _ = x.shape
    mem_n = rmsnorm(mem, mem_norm)
    h = x
    shared_k = shared_v = shared_c = None
    for l in range(DEPTH):
        h = h + 0.5 * swiglu(rmsnorm(h, ffn1_norm[l]), ffn1_w_gate_up[l], ffn1_w_down[l])
        u = rmsnorm(h, mix_norm[l])
        if l < N_A_LAYERS:
            main, q_mem = gdn_mixer(u, gdn_w_in[l], gdn_conv[l], gdn_A_log[l],
                                    gdn_dt_bias[l], gdn_out_norm[l])
        else:
            proj = u @ fox_w_in[l - N_A_LAYERS]
            q_fox = proj[..., :FOX_WIDTH].reshape(bsz, s, FOX_HEADS, HEAD_DIM)
            q_mem = proj[..., FOX_WIDTH:]
            main = forgetting_attention(q_fox, shared_k, shared_v, shared_c)
        mem_out = memory_attention(q_mem, mem_n, mem_w_kv[l])
        h = h + jnp.concatenate([main, mem_out], axis=-1) @ w_out[l]
        h = h + 0.5 * swiglu(rmsnorm(h, ffn2_norm[l]), ffn2_w_gate_up[l], ffn2_w_down[l])
        if l == N_A_LAYERS - 1:
            shared_k, shared_v, shared_c = shared_fox_kv(h, kv_norm, kv_w, kv_b_f)
    return rmsnorm(h, final_norm)
```

```python
import functools

import jax
import jax.numpy as jnp
from jax import lax
from jax.experimental import pallas as pl
from jax.experimental.pallas import tpu as pltpu

F32 = jnp.float32
BF16 = jnp.bfloat16

EPS = 1e-6
NEG_INF = -1e30
HEAD_DIM = 128
GDN_HEADS = 6
GDN_WIDTH = GDN_HEADS * HEAD_DIM
FOX_HEADS = 6
FOX_WIDTH = FOX_HEADS * HEAD_DIM
CONV_WIDTH = 4
MEM_HEADS = 4
MEM_HEAD_DIM = 64
MEM_WIDTH = MEM_HEADS * MEM_HEAD_DIM

LANES = 128
SUBLANES = 8
GDN_CHUNK = 128
VMEM_LIMIT = 56 * 1024 * 1024


def _params(*sem):
    return pltpu.CompilerParams(dimension_semantics=sem, vmem_limit_bytes=VMEM_LIMIT)


def _dot(a, b):
    return jnp.dot(a, b, preferred_element_type=F32)


def _dot_nt(a, b):
    return lax.dot_general(a, b, (((1,), (1,)), ((), ())), preferred_element_type=F32)


def _rms(x, gain):
    return x * lax.rsqrt(jnp.mean(x * x, axis=-1, keepdims=True) + EPS) * gain


def _split3(x):
    hi = x.astype(BF16)
    r1 = x - hi.astype(F32)
    mid = r1.astype(BF16)
    lo = (r1 - mid.astype(F32)).astype(BF16)
    return hi, mid, lo


def _tri_cumsum(tri, x):
    hi, mid, lo = _split3(x)
    return _dot(tri, hi) + _dot(tri, mid) + _dot(tri, lo)


def _norm_matmul_kernel(x_ref, g_ref, w_ref, *out_refs, splits):
    xn = _rms(x_ref[...], g_ref[...]).astype(BF16)
    for (lo, hi, transposed), o_ref in zip(splits, out_refs):
        y = _dot(xn, w_ref[:, lo:hi])
        if transposed:
            o_ref[...] = y.T.astype(o_ref.dtype)
        else:
            o_ref[...] = y.astype(o_ref.dtype)


def norm_matmul(x, gain, w, splits, dtypes, tm, name):
    t, d = x.shape
    n = w.shape[1]
    out_shape, out_specs = [], []
    for (lo, hi, transposed), dt in zip(splits, dtypes):
        if transposed:
            out_shape.append(jax.ShapeDtypeStruct((hi - lo, t), dt))
            out_specs.append(pl.BlockSpec((hi - lo, tm), lambda i: (0, i)))
        else:
            out_shape.append(jax.ShapeDtypeStruct((t, hi - lo), dt))
            out_specs.append(pl.BlockSpec((tm, hi - lo), lambda i: (i, 0)))
    return pl.pallas_call(
        functools.partial(_norm_matmul_kernel, splits=tuple(splits)),
        grid=(t // tm,),
        in_specs=[pl.BlockSpec((tm, d), lambda i: (i, 0)),
                  pl.BlockSpec((1, d), lambda i: (0, 0)),
                  pl.BlockSpec((d, n), lambda i: (0, 0))],
        out_specs=out_specs,
        out_shape=out_shape,
        compiler_params=_params("parallel"),
        name=name,
    )(x, gain.reshape(1, d), w)


def _ffn_kernel(h_ref, g_ref, wgu_ref, wd_ref, fg_ref, o_ref, act_ref, *, hidden, tf, final_norm):
    x = h_ref[...]
    xn = _rms(x, g_ref[...]).astype(BF16)
    for j in range(hidden // tf):
        gate = _dot(xn, wgu_ref[:, j * tf:(j + 1) * tf])
        up = _dot(xn, wgu_ref[:, hidden + j * tf:hidden + (j + 1) * tf])
        act_ref[:, j * tf:(j + 1) * tf] = (gate * jax.nn.sigmoid(gate) * up).astype(BF16)
    out = x + 0.5 * _dot(act_ref[...], wd_ref[...])
    if final_norm:
        out = _rms(out, fg_ref[...])
    o_ref[...] = out


def ffn(h, gain, wgu, wd, final_gain, tm, tf, final_norm, name):
    t, d = h.shape
    hidden = wd.shape[0]
    return pl.pallas_call(
        functools.partial(_ffn_kernel, hidden=hidden, tf=tf, final_norm=final_norm),
        grid=(t // tm,),
        in_specs=[pl.BlockSpec((tm, d), lambda i: (i, 0)),
                  pl.BlockSpec((1, d), lambda i: (0, 0)),
                  pl.BlockSpec((d, 2 * hidden), lambda i: (0, 0), pipeline_mode=pl.Buffered(1)),
                  pl.BlockSpec((hidden, d), lambda i: (0, 0), pipeline_mode=pl.Buffered(1)),
                  pl.BlockSpec((1, d), lambda i: (0, 0))],
        out_specs=pl.BlockSpec((tm, d), lambda i: (i, 0)),
        out_shape=jax.ShapeDtypeStruct((t, d), F32),
        scratch_shapes=[pltpu.VMEM((tm, hidden), BF16)],
        compiler_params=_params("parallel"),
        name=name,
    )(h, gain.reshape(1, d), wgu, wd, final_gain.reshape(1, d))


def _unit_lower_inverse(a):
    n = a.shape[0]
    row = lax.broadcasted_iota(jnp.int32, (n, n), 0)
    col = lax.broadcasted_iota(jnp.int32, (n, n), 1)
    eye = (row == col).astype(F32)
    t = eye - jnp.where((row >> 1) == (col >> 1), a, 0.0)
    s = 2
    while s < n:
        k = s.bit_length() - 1
        sub = ((row >> (k + 1)) == (col >> (k + 1))) & ((row >> k) != (col >> k))
        e = jnp.where(sub, a, 0.0).astype(BF16)
        tb = t.astype(BF16)
        t = t - _dot(_dot(tb, e).astype(BF16), tb)
        s *= 2
    return t


def _gdn_prep_kernel(x_ref, xp_ref, ab_ref, cw_ref, alog_ref, dtb_ref,
                     u_ref, w_ref, qh_ref, kt_ref, qk_ref, eg_ref, *, blocks_per_seq):
    c = GDN_CHUNK
    nchunk = x_ref.shape[0] // c
    i = pl.program_id(0)
    first = (i % blocks_per_seq) == 0
    x = x_ref[...]
    prev = jnp.where(first, 0.0, xp_ref[...])
    xs = jnp.concatenate([prev, x], axis=0)
    ln = x.shape[0]
    y = cw_ref[CONV_WIDTH - 1:CONV_WIDTH, :] * x
    for tap in range(CONV_WIDTH - 1):
        shift = CONV_WIDTH - 1 - tap
        y = y + cw_ref[tap:tap + 1, :] * xs[SUBLANES - shift:SUBLANES - shift + ln, :]
    y = y * jax.nn.sigmoid(y)

    ab = ab_ref[...]
    pre = ab + dtb_ref[...]
    softplus = jnp.maximum(pre, 0.0) + jnp.log(1.0 + jnp.exp(-jnp.abs(pre)))
    g_col = -jnp.exp(alog_ref[...]) * softplus
    beta_col = jax.nn.sigmoid(ab)

    row = lax.broadcasted_iota(jnp.int32, (c, c), 0)
    col = lax.broadcasted_iota(jnp.int32, (c, c), 1)
    lower = row >= col
    tri = lower.astype(BF16)

    for ci in range(nchunk):
        r0 = ci * c
        gc_col = _tri_cumsum(tri, g_col[r0:r0 + c, :])
        gc_row = gc_col.T
        eg_ref[ci] = jnp.exp(jnp.broadcast_to(gc_row[0:2 * SUBLANES, c - 1:c], (2 * SUBLANES, LANES)))
        for h in range(GDN_HEADS):
            q = y[r0:r0 + c, h * HEAD_DIM:(h + 1) * HEAD_DIM]
            k = y[r0:r0 + c, GDN_WIDTH + h * HEAD_DIM:GDN_WIDTH + (h + 1) * HEAD_DIM]
            v = y[r0:r0 + c, 2 * GDN_WIDTH + h * HEAD_DIM:2 * GDN_WIDTH + (h + 1) * HEAD_DIM]
            q = q * lax.rsqrt(jnp.sum(q * q, axis=-1, keepdims=True) + EPS) * (HEAD_DIM ** -0.5)
            k = k * lax.rsqrt(jnp.sum(k * k, axis=-1, keepdims=True) + EPS)
            beta = beta_col[r0:r0 + c, GDN_HEADS + h:GDN_HEADS + h + 1]
            gcl = gc_col[:, h:h + 1]
            gr = gc_row[h:h + 1, :]
            glast = gr[:, c - 1:c]
            decay = jnp.where(lower, jnp.exp(jnp.where(lower, gcl - gr, 0.0)), 0.0)
            kb = k * beta
            kbf = k.astype(BF16)
            a = jnp.where(row > col, _dot_nt(kb.astype(BF16), kbf) * decay, 0.0)
            qk = _dot_nt(q.astype(BF16), kbf) * decay
            t = _unit_lower_inverse(a).astype(BF16)
            egc = jnp.exp(gcl)
            rhs = jnp.concatenate([v * beta, kb * egc], axis=1).astype(BF16)
            sol = _dot(t, rhs)
            cs = slice(h * HEAD_DIM, (h + 1) * HEAD_DIM)
            u_ref[r0:r0 + c, cs] = sol[:, :HEAD_DIM]
            w_ref[r0:r0 + c, cs] = sol[:, HEAD_DIM:].astype(BF16)
            qh_ref[r0:r0 + c, cs] = (q * egc).astype(BF16)
            qk_ref[r0:r0 + c, cs] = qk.astype(BF16)
            kt_ref[cs, r0:r0 + c] = (k * jnp.exp(glast - gcl)).T.astype(BF16)


def gdn_prep(qkv, ab, conv_w, alog_row, dtb_row, seq, tl):
    t = qkv.shape[0]
    w3 = qkv.shape[1]
    width = w3 // 3
    nblk = t // tl
    prev_blocks = tl // SUBLANES
    tok = lambda i: (i, 0)
    const = lambda i: (0, 0)
    return pl.pallas_call(
        functools.partial(_gdn_prep_kernel, blocks_per_seq=seq // tl),
        grid=(nblk,),
        in_specs=[pl.BlockSpec((tl, w3), tok),
                  pl.BlockSpec((SUBLANES, w3), lambda i: (jnp.maximum(i * prev_blocks - 1, 0), 0)),
                  pl.BlockSpec((tl, LANES), tok),
                  pl.BlockSpec((CONV_WIDTH, w3), const),
                  pl.BlockSpec((1, LANES), const),
                  pl.BlockSpec((1, LANES), const)],
        out_specs=[pl.BlockSpec((tl, width), tok),
                   pl.BlockSpec((tl, width), tok),
                   pl.BlockSpec((tl, width), tok),
                   pl.BlockSpec((width, tl), lambda i: (0, i)),
                   pl.BlockSpec((tl, width), tok),
                   pl.BlockSpec((tl // GDN_CHUNK, 2 * SUBLANES, LANES), lambda i: (i, 0, 0))],
        out_shape=[jax.ShapeDtypeStruct((t, width), F32),
                   jax.ShapeDtypeStruct((t, width), BF16),
                   jax.ShapeDtypeStruct((t, width), BF16),
                   jax.ShapeDtypeStruct((width, t), BF16),
                   jax.ShapeDtypeStruct((t, width), BF16),
                   jax.ShapeDtypeStruct((t // GDN_CHUNK, 2 * SUBLANES, LANES), F32)],
        compiler_params=_params("parallel"),
        name="gdn_prep",
    )(qkv, qkv, ab, conv_w, alog_row, dtb_row)


def _gdn_scan_kernel(u_ref, w_ref, qh_ref, kt_ref, qk_ref, eg_ref, z_ref, on_ref, o_ref, s_ref):
    c = GDN_CHUNK
    nchunk = u_ref.shape[0] // c

    @pl.when(pl.program_id(1) == 0)
    def _():
        s_ref[...] = jnp.zeros_like(s_ref)

    gain = on_ref[...]
    for ci in range(nchunk):
        r0 = ci * c
        eg = eg_ref[ci]
        for h in range(GDN_HEADS):
            cs = slice(h * HEAD_DIM, (h + 1) * HEAD_DIM)
            state = s_ref[h]
            sb = state.astype(BF16)
            v_new = u_ref[r0:r0 + c, cs] - _dot(w_ref[r0:r0 + c, cs], sb)
            vb = v_new.astype(BF16)
            out = _dot(qh_ref[r0:r0 + c, cs], sb) + _dot(qk_ref[r0:r0 + c, cs], vb)
            s_ref[h] = state * eg[h:h + 1, :] + _dot(kt_ref[cs, r0:r0 + c], vb)
            zz = z_ref[r0:r0 + c, cs]
            o_ref[r0:r0 + c, cs] = (_rms(out, gain) * (zz * jax.nn.sigmoid(zz))).astype(o_ref.dtype)


def gdn_scan(u, w, qh, kt, qk, eg, z, out_norm, batch, tl):
    t, width = u.shape
    nblk = t // batch // tl
    tok = lambda b, i: (b * nblk + i, 0)
    return pl.pallas_call(
        _gdn_scan_kernel,
        grid=(batch, nblk),
        in_specs=[pl.BlockSpec((tl, width), tok),
                  pl.BlockSpec((tl, width), tok),
                  pl.BlockSpec((tl, width), tok),
                  pl.BlockSpec((width, tl), lambda b, i: (0, b * nblk + i)),
                  pl.BlockSpec((tl, width), tok),
                  pl.BlockSpec((tl // GDN_CHUNK, 2 * SUBLANES, LANES), lambda b, i: (b * nblk + i, 0, 0)),
                  pl.BlockSpec((tl, width), tok),
                  pl.BlockSpec((1, HEAD_DIM), lambda b, i: (0, 0))],
        out_specs=pl.BlockSpec((tl, width), tok),
        out_shape=jax.ShapeDtypeStruct((t, width), BF16),
        scratch_shapes=[pltpu.VMEM((GDN_HEADS, HEAD_DIM, HEAD_DIM), F32)],
        compiler_params=_params("parallel", "arbitrary"),
        name="gdn_scan",
    )(u, w, qh, kt, qk, eg, z, out_norm.reshape(1, HEAD_DIM))


def _forget_cumsum_kernel(f_ref, b_ref, ctm_ref, cfm_ref, carry_ref):
    @pl.when(pl.program_id(1) == 0)
    def _():
        carry_ref[...] = jnp.zeros_like(carry_ref)

    n = f_ref.shape[0]
    pre = f_ref[...] + b_ref[...]
    logf = jnp.minimum(pre, 0.0) - jnp.log(1.0 + jnp.exp(-jnp.abs(pre)))
    row = lax.broadcasted_iota(jnp.int32, (n, n), 0)
    col = lax.broadcasted_iota(jnp.int32, (n, n), 1)
    cs = _tri_cumsum((row >= col).astype(BF16), logf) + carry_ref[...]
    carry_ref[...] = cs[n - 1:n, :]
    ctm_ref[...] = cs
    cfm_ref[0] = cs.T[:SUBLANES, :]


def forget_cumsum(flog, bias_row, batch, tl):
    t = flog.shape[0]
    seq = t // batch
    nblk = seq // tl
    return pl.pallas_call(
        _forget_cumsum_kernel,
        grid=(batch, nblk),
        in_specs=[pl.BlockSpec((tl, LANES), lambda b, i: (b * nblk + i, 0)),
                  pl.BlockSpec((1, LANES), lambda b, i: (0, 0))],
        out_specs=[pl.BlockSpec((tl, LANES), lambda b, i: (b * nblk + i, 0)),
                   pl.BlockSpec((1, SUBLANES, tl), lambda b, i: (b, 0, i))],
        out_shape=[jax.ShapeDtypeStruct((t, LANES), F32),
                   jax.ShapeDtypeStruct((batch, SUBLANES, seq), F32)],
        scratch_shapes=[pltpu.VMEM((1, LANES), F32)],
        compiler_params=_params("parallel", "arbitrary"),
        name="forget_cumsum",
    )(flog, bias_row)


def _fox_kernel(q_ref, kt_ref, v_ref, ctm_ref, cfm_ref, o_ref, m_ref, l_ref, acc_ref, *, tq, tk):
    i = pl.program_id(1)
    j = pl.program_id(2)

    @pl.when(j == 0)
    def _():
        m_ref[...] = jnp.full_like(m_ref, NEG_INF)
        l_ref[...] = jnp.zeros_like(l_ref)
        acc_ref[...] = jnp.zeros_like(acc_ref)

    @pl.when(j <= i)
    def _():
        q_pos = i * tq + lax.broadcasted_iota(jnp.int32, (tq, tk), 0)
        k_pos = j * tk + lax.broadcasted_iota(jnp.int32, (tq, tk), 1)
        causal = k_pos <= q_pos
        scale = HEAD_DIM ** -0.5
        for h in range(FOX_HEADS):
            cs = slice(h * HEAD_DIM, (h + 1) * HEAD_DIM)
            s = _dot(q_ref[:, cs], kt_ref[cs, :]) * scale
            s = s + (ctm_ref[:, h:h + 1] - cfm_ref[0, h:h + 1, :])
            s = jnp.where(causal, s, NEG_INF)
            m_prev = m_ref[h]
            m_next = jnp.maximum(m_prev, jnp.max(s, axis=1, keepdims=True))
            alpha = jnp.exp(m_prev - m_next)
            p = jnp.exp(s - pltpu.repeat(m_next, tk // LANES, axis=1))
            l_ref[h] = alpha * l_ref[h] + jnp.sum(p, axis=1, keepdims=True)
            acc_ref[h] = alpha * acc_ref[h] + _dot(p.astype(BF16), v_ref[:, cs])
            m_ref[h] = m_next

    @pl.when(j == i)
    def _():
        for h in range(FOX_HEADS):
            cs = slice(h * HEAD_DIM, (h + 1) * HEAD_DIM)
            o_ref[:, cs] = (acc_ref[h] / l_ref[h]).astype(o_ref.dtype)


def fox_attention(q, kt, v, ctm, cfm, batch, tq):
    t, width = q.shape
    seq = t // batch
    nq = seq // tq
    tk = tq
    kcol = lambda b, i, j: b * nq + jnp.minimum(j, i)
    return pl.pallas_call(
        functools.partial(_fox_kernel, tq=tq, tk=tk),
        grid=(batch, nq, nq),
        in_specs=[pl.BlockSpec((tq, width), lambda b, i, j: (b * nq + i, 0)),
                  pl.BlockSpec((width, tk), lambda b, i, j: (0, kcol(b, i, j))),
                  pl.BlockSpec((tk, width), lambda b, i, j: (kcol(b, i, j), 0)),
                  pl.BlockSpec((tq, LANES), lambda b, i, j: (b * nq + i, 0)),
                  pl.BlockSpec((1, SUBLANES, tk), lambda b, i, j: (b, 0, jnp.minimum(j, i)))],
        out_specs=pl.BlockSpec((tq, width), lambda b, i, j: (b * nq + i, 0)),
        out_shape=jax.ShapeDtypeStruct((t, width), BF16),
        scratch_shapes=[pltpu.VMEM((FOX_HEADS, tq, LANES), F32),
                        pltpu.VMEM((FOX_HEADS, tq, LANES), F32),
                        pltpu.VMEM((FOX_HEADS, tq, HEAD_DIM), F32)],
        compiler_params=_params("parallel", "parallel", "arbitrary"),
        name="fox_attention",
    )(q, kt, v, ctm, cfm)


def _mix_out_kernel(h_ref, main_ref, qm_ref, kbd_ref, vbd_ref, wo_ref, o_ref, *, main_width, mem_len):
    s = _dot(qm_ref[...], kbd_ref[0]) * (MEM_HEAD_DIM ** -0.5)
    probs = []
    for g in range(MEM_HEADS):
        sg = s[:, g * mem_len:(g + 1) * mem_len]
        p = jnp.exp(sg - jnp.max(sg, axis=1, keepdims=True))
        probs.append((p / jnp.sum(p, axis=1, keepdims=True)).astype(BF16))
    mem_out = _dot(jnp.concatenate(probs, axis=1), vbd_ref[0])
    o_ref[...] = (h_ref[...] + _dot(main_ref[...], wo_ref[:main_width, :])
                  + _dot(mem_out.astype(BF16), wo_ref[main_width:, :]))


def mix_out(h, main, qmem, kbd, vbd, wo, batch, tm):
    t, d = h.shape
    main_width = main.shape[1]
    mem_len = kbd.shape[2] // MEM_HEADS
    per_batch = t // batch // tm
    tok = lambda i: (i, 0)
    return pl.pallas_call(
        functools.partial(_mix_out_kernel, main_width=main_width, mem_len=mem_len),
        grid=(t // tm,),
        in_specs=[pl.BlockSpec((tm, d), tok),
                  pl.BlockSpec((tm, main_width), tok),
                  pl.BlockSpec((tm, MEM_WIDTH), tok),
                  pl.BlockSpec((1,) + kbd.shape[1:], lambda i: (i // per_batch, 0, 0)),
                  pl.BlockSpec((1,) + vbd.shape[1:], lambda i: (i // per_batch, 0, 0)),
                  pl.BlockSpec(wo.shape, lambda i: (0, 0))],
        out_specs=pl.BlockSpec((tm, d), tok),
        out_shape=jax.ShapeDtypeStruct((t, d), F32),
        compiler_params=_params("parallel"),
        name="mix_out",
    )(h, main, qmem, kbd, vbd, wo)


def _pad_cols(w, n):
    return jnp.pad(w, ((0, 0), (0, n - w.shape[1])))


def _lane_row(vec, offset=0):
    return jnp.zeros((1, LANES), F32).at[0, offset:offset + vec.shape[0]].set(vec.astype(F32))


def _memory_block_diag(kv, batch, mem_len):
    kv = kv.reshape(batch, mem_len, 2 * MEM_WIDTH)
    k, v = kv[..., :MEM_WIDTH], kv[..., MEM_WIDTH:]
    head_of_feat = jnp.arange(MEM_WIDTH) // MEM_HEAD_DIM
    head_of_slot = jnp.arange(MEM_HEADS * mem_len) // mem_len
    mask = head_of_feat[:, None] == head_of_slot[None, :]
    kbd = jnp.where(mask[None], jnp.tile(k.transpose(0, 2, 1), (1, 1, MEM_HEADS)), 0).astype(BF16)
    vbd = jnp.where(mask.T[None], jnp.tile(v, (1, MEM_HEADS, 1)), 0).astype(BF16)
    return kbd, vbd


def kernel(x, mem, ffn1_norm, ffn1_w_gate_up, ffn1_w_down, mix_norm, ffn2_norm, ffn2_w_gate_up,
           ffn2_w_down, gdn_w_in, gdn_conv, gdn_A_log, gdn_dt_bias, gdn_out_norm, fox_w_in, w_out,
           mem_norm, mem_w_kv, kv_norm, kv_w, kv_b_f, final_norm):
    bsz, seq, d = x.shape
    mem_len = mem.shape[1]
    depth = ffn1_norm.shape[0]
    n_a = gdn_w_in.shape[0]
    t = bsz * seq
    tm = min(512, seq)
    tf = 256
    tl = min(256, seq)
    tq = min(512, seq)

    h = x.reshape(t, d).astype(F32)

    w_kv_all = jnp.concatenate([mem_w_kv[l] for l in range(depth)], axis=1).astype(BF16)
    (kv_mem,) = norm_matmul(mem.reshape(bsz * mem_len, d).astype(F32), mem_norm, w_kv_all,
                            [(0, w_kv_all.shape[1], False)], [F32], min(256, bsz * mem_len), "mem_kv")

    o0, o1 = 3 * GDN_WIDTH, 4 * GDN_WIDTH
    kt = v_sh = ctm = cfm = None
    for l in range(depth):
        last = l == depth - 1
        h = ffn(h, ffn1_norm[l], ffn1_w_gate_up[l].astype(BF16), ffn1_w_down[l].astype(BF16),
                final_norm, tm, tf, False, "ffn1")
        kbd, vbd = _memory_block_diag(kv_mem[:, 2 * MEM_WIDTH * l:2 * MEM_WIDTH * (l + 1)], bsz, mem_len)
        if l < n_a:
            w_in = gdn_w_in[l]
            w_cat = jnp.concatenate([w_in[:, :o1], _pad_cols(w_in[:, o1:o1 + 2 * GDN_HEADS], LANES),
                                     w_in[:, o1 + 2 * GDN_HEADS:]], axis=1).astype(BF16)
            qkv, z, ab, qmem = norm_matmul(
                h, mix_norm[l], w_cat,
                [(0, o0, False), (o0, o1, False), (o1, o1 + LANES, False),
                 (o1 + LANES, o1 + LANES + MEM_WIDTH, False)],
                [F32, F32, F32, BF16], tm, "gdn_in")
            alog_row = _lane_row(gdn_A_log[l])
            dtb_row = _lane_row(gdn_dt_bias[l])
            u, w, qh, ktl, qk, eg = gdn_prep(qkv, ab, gdn_conv[l].astype(F32), alog_row, dtb_row, seq, tl)
            main = gdn_scan(u, w, qh, ktl, qk, eg, z, gdn_out_norm[l].astype(F32), bsz, tl)
        else:
            w_in = fox_w_in[l - n_a].astype(BF16)
            q_fox, qmem = norm_matmul(h, mix_norm[l], w_in,
                                      [(0, FOX_WIDTH, False), (FOX_WIDTH, FOX_WIDTH + MEM_WIDTH, False)],
                                      [BF16, BF16], tm, "fox_in")
            main = fox_attention(q_fox, kt, v_sh, ctm, cfm, bsz, tq)
        h = mix_out(h, main, qmem, kbd, vbd, w_out[l].astype(BF16), bsz, tm)
        h = ffn(h, ffn2_norm[l], ffn2_w_gate_up[l].astype(BF16), ffn2_w_down[l].astype(BF16),
                final_norm, tm, tf, last, "ffn2")
        if l == n_a - 1:
            w_cat = jnp.concatenate([kv_w[:, :2 * FOX_WIDTH], _pad_cols(kv_w[:, 2 * FOX_WIDTH:], LANES)],
                                    axis=1).astype(BF16)
            kt, v_sh, flog = norm_matmul(
                h, kv_norm, w_cat,
                [(0, FOX_WIDTH, True), (FOX_WIDTH, 2 * FOX_WIDTH, False),
                 (2 * FOX_WIDTH, 2 * FOX_WIDTH + LANES, False)],
                [BF16, BF16, F32], tm, "shared_kv")
            ctm, cfm = forget_cumsum(flog, _lane_row(kv_b_f), bsz, tm)
    return h.reshape(bsz, seq, d)
```

```python
import functools

import jax
import jax.numpy as jnp
from jax import lax
from jax.experimental import pallas as pl
from jax.experimental.pallas import tpu as pltpu

F32 = jnp.float32
BF16 = jnp.bfloat16

EPS = 1e-6
NEG_INF = -1e30
HEAD_DIM = 128
GDN_HEADS = 6
GDN_WIDTH = GDN_HEADS * HEAD_DIM
FOX_HEADS = 6
FOX_WIDTH = FOX_HEADS * HEAD_DIM
CONV_WIDTH = 4
MEM_HEADS = 4
MEM_HEAD_DIM = 64
MEM_WIDTH = MEM_HEADS * MEM_HEAD_DIM

LANES = 128
SUBLANES = 8
GDN_CHUNK = 128
FOX_AUG = 16
LOG2E = 1.4426950408889634
VMEM_LIMIT = 56 * 1024 * 1024


def _params(*sem):
    return pltpu.CompilerParams(dimension_semantics=sem, vmem_limit_bytes=VMEM_LIMIT)


def _dot(a, b):
    return jnp.dot(a, b, preferred_element_type=F32)


def _dot_nt(a, b):
    return lax.dot_general(a, b, (((1,), (1,)), ((), ())), preferred_element_type=F32)


def _rms(x, gain):
    return x * lax.rsqrt(jnp.mean(x * x, axis=-1, keepdims=True) + EPS) * gain


def _split3(x):
    hi = x.astype(BF16)
    r1 = x - hi.astype(F32)
    mid = r1.astype(BF16)
    lo = (r1 - mid.astype(F32)).astype(BF16)
    return hi, mid, lo


def _tri_cumsum(tri, x):
    hi, mid, lo = _split3(x)
    return _dot(tri, hi) + _dot(tri, mid) + _dot(tri, lo)


def _norm_matmul_kernel(x_ref, g_ref, w_ref, *out_refs, splits):
    xn = _rms(x_ref[...], g_ref[...]).astype(BF16)
    for (lo, hi, transposed), o_ref in zip(splits, out_refs):
        y = _dot(xn, w_ref[:, lo:hi])
        if transposed:
            o_ref[...] = y.T.astype(o_ref.dtype)
        else:
            o_ref[...] = y.astype(o_ref.dtype)


def norm_matmul(x, gain, w, splits, dtypes, tm, name):
    t, d = x.shape
    n = w.shape[1]
    out_shape, out_specs = [], []
    for (lo, hi, transposed), dt in zip(splits, dtypes):
        if transposed:
            out_shape.append(jax.ShapeDtypeStruct((hi - lo, t), dt))
            out_specs.append(pl.BlockSpec((hi - lo, tm), lambda i: (0, i)))
        else:
            out_shape.append(jax.ShapeDtypeStruct((t, hi - lo), dt))
            out_specs.append(pl.BlockSpec((tm, hi - lo), lambda i: (i, 0)))
    return pl.pallas_call(
        functools.partial(_norm_matmul_kernel, splits=tuple(splits)),
        grid=(t // tm,),
        in_specs=[pl.BlockSpec((tm, d), lambda i: (i, 0)),
                  pl.BlockSpec((1, d), lambda i: (0, 0)),
                  pl.BlockSpec((d, n), lambda i: (0, 0))],
        out_specs=out_specs,
        out_shape=out_shape,
        compiler_params=_params("parallel"),
        name=name,
    )(x, gain.reshape(1, d), w)


def _ffn_kernel(h_ref, g_ref, wgu_ref, wd_ref, fg_ref, o_ref, act_ref, *, hidden, tf, final_norm):
    x = h_ref[...]
    xn = _rms(x, g_ref[...]).astype(BF16)
    for j in range(hidden // tf):
        gate = _dot(xn, wgu_ref[:, j * tf:(j + 1) * tf])
        up = _dot(xn, wgu_ref[:, hidden + j * tf:hidden + (j + 1) * tf])
        act_ref[:, j * tf:(j + 1) * tf] = (gate * jax.nn.sigmoid(gate) * up).astype(BF16)
    out = x + 0.5 * _dot(act_ref[...], wd_ref[...])
    if final_norm:
        out = _rms(out, fg_ref[...])
    o_ref[...] = out


def ffn(h, gain, wgu, wd, final_gain, tm, tf, final_norm, name):
    t, d = h.shape
    hidden = wd.shape[0]
    return pl.pallas_call(
        functools.partial(_ffn_kernel, hidden=hidden, tf=tf, final_norm=final_norm),
        grid=(t // tm,),
        in_specs=[pl.BlockSpec((tm, d), lambda i: (i, 0)),
                  pl.BlockSpec((1, d), lambda i: (0, 0)),
                  pl.BlockSpec((d, 2 * hidden), lambda i: (0, 0), pipeline_mode=pl.Buffered(1)),
                  pl.BlockSpec((hidden, d), lambda i: (0, 0), pipeline_mode=pl.Buffered(1)),
                  pl.BlockSpec((1, d), lambda i: (0, 0))],
        out_specs=pl.BlockSpec((tm, d), lambda i: (i, 0)),
        out_shape=jax.ShapeDtypeStruct((t, d), F32),
        scratch_shapes=[pltpu.VMEM((tm, hidden), BF16)],
        compiler_params=_params("parallel"),
        name=name,
    )(h, gain.reshape(1, d), wgu, wd, final_gain.reshape(1, d))


def _unit_lower_inverses(mats):
    n = mats[0].shape[0]
    row = lax.broadcasted_iota(jnp.int32, (n, n), 0)
    col = lax.broadcasted_iota(jnp.int32, (n, n), 1)
    eye = (row == col).astype(F32)
    pair = (row >> 1) == (col >> 1)
    ts = [eye - jnp.where(pair, a, 0.0) for a in mats]
    s = 2
    while s < n:
        k = s.bit_length() - 1
        sub = ((row >> (k + 1)) == (col >> (k + 1))) & ((row >> k) != (col >> k))
        tbs = [t.astype(BF16) for t in ts]
        tes = [_dot(tb, jnp.where(sub, a, 0.0).astype(BF16)).astype(BF16) for tb, a in zip(tbs, mats)]
        ts = [t - _dot(te, tb) for t, te, tb in zip(ts, tes, tbs)]
        s *= 2
    return ts


def _gdn_prep_kernel(x_ref, xp_ref, ab_ref, cw_ref, alog_ref, dtb_ref,
                     u_ref, w_ref, qh_ref, kt_ref, qk_ref, eg_ref, *, blocks_per_seq):
    c = GDN_CHUNK
    nchunk = x_ref.shape[0] // c
    i = pl.program_id(0)
    first = (i % blocks_per_seq) == 0
    x = x_ref[...]
    prev = jnp.where(first, 0.0, xp_ref[...])
    xs = jnp.concatenate([prev, x], axis=0)
    ln = x.shape[0]
    y = cw_ref[CONV_WIDTH - 1:CONV_WIDTH, :] * x
    for tap in range(CONV_WIDTH - 1):
        shift = CONV_WIDTH - 1 - tap
        y = y + cw_ref[tap:tap + 1, :] * xs[SUBLANES - shift:SUBLANES - shift + ln, :]
    y = y * jax.nn.sigmoid(y)

    ab = ab_ref[...]
    pre = ab + dtb_ref[...]
    softplus = jnp.maximum(pre, 0.0) + jnp.log(1.0 + jnp.exp(-jnp.abs(pre)))
    g_col = -jnp.exp(alog_ref[...]) * softplus
    beta_col = jax.nn.sigmoid(ab)

    row = lax.broadcasted_iota(jnp.int32, (c, c), 0)
    col = lax.broadcasted_iota(jnp.int32, (c, c), 1)
    lower = row >= col
    tri = lower.astype(BF16)

    mats, rhss, where = [], [], []
    for ci in range(nchunk):
        r0 = ci * c
        gc_col = _tri_cumsum(tri, g_col[r0:r0 + c, :])
        gc_row = gc_col.T
        eg_ref[ci] = jnp.exp(jnp.broadcast_to(gc_row[0:2 * SUBLANES, c - 1:c], (2 * SUBLANES, LANES)))
        for h in range(GDN_HEADS):
            q = y[r0:r0 + c, h * HEAD_DIM:(h + 1) * HEAD_DIM]
            k = y[r0:r0 + c, GDN_WIDTH + h * HEAD_DIM:GDN_WIDTH + (h + 1) * HEAD_DIM]
            v = y[r0:r0 + c, 2 * GDN_WIDTH + h * HEAD_DIM:2 * GDN_WIDTH + (h + 1) * HEAD_DIM]
            q = q * lax.rsqrt(jnp.sum(q * q, axis=-1, keepdims=True) + EPS) * (HEAD_DIM ** -0.5)
            k = k * lax.rsqrt(jnp.sum(k * k, axis=-1, keepdims=True) + EPS)
            beta = beta_col[r0:r0 + c, GDN_HEADS + h:GDN_HEADS + h + 1]
            gcl = gc_col[:, h:h + 1]
            gr = gc_row[h:h + 1, :]
            glast = gr[:, c - 1:c]
            decay = jnp.where(lower, jnp.exp(jnp.where(lower, gcl - gr, 0.0)), 0.0)
            kb = k * beta
            kbf = k.astype(BF16)
            mats.append(jnp.where(row > col, _dot_nt(kb.astype(BF16), kbf) * decay, 0.0))
            qk = _dot_nt(q.astype(BF16), kbf) * decay
            egc = jnp.exp(gcl)
            rhss.append(jnp.concatenate([v * beta, kb * egc], axis=1).astype(BF16))
            cs = slice(h * HEAD_DIM, (h + 1) * HEAD_DIM)
            where.append((r0, cs))
            qh_ref[r0:r0 + c, cs] = (q * egc).astype(BF16)
            qk_ref[r0:r0 + c, cs] = qk.astype(BF16)
            kt_ref[cs, r0:r0 + c] = (k * jnp.exp(glast - gcl)).T.astype(BF16)

    for t, rhs, (r0, cs) in zip(_unit_lower_inverses(mats), rhss, where):
        sol = _dot(t.astype(BF16), rhs)
        u_ref[r0:r0 + c, cs] = sol[:, :HEAD_DIM]
        w_ref[r0:r0 + c, cs] = sol[:, HEAD_DIM:].astype(BF16)


def gdn_prep(qkv, ab, conv_w, alog_row, dtb_row, seq, tl):
    t = qkv.shape[0]
    w3 = qkv.shape[1]
    width = w3 // 3
    nblk = t // tl
    prev_blocks = tl // SUBLANES
    tok = lambda i: (i, 0)
    const = lambda i: (0, 0)
    return pl.pallas_call(
        functools.partial(_gdn_prep_kernel, blocks_per_seq=seq // tl),
        grid=(nblk,),
        in_specs=[pl.BlockSpec((tl, w3), tok),
                  pl.BlockSpec((SUBLANES, w3), lambda i: (jnp.maximum(i * prev_blocks - 1, 0), 0)),
                  pl.BlockSpec((tl, LANES), tok),
                  pl.BlockSpec((CONV_WIDTH, w3), const),
                  pl.BlockSpec((1, LANES), const),
                  pl.BlockSpec((1, LANES), const)],
        out_specs=[pl.BlockSpec((tl, width), tok),
                   pl.BlockSpec((tl, width), tok),
                   pl.BlockSpec((tl, width), tok),
                   pl.BlockSpec((width, tl), lambda i: (0, i)),
                   pl.BlockSpec((tl, width), tok),
                   pl.BlockSpec((tl // GDN_CHUNK, 2 * SUBLANES, LANES), lambda i: (i, 0, 0))],
        out_shape=[jax.ShapeDtypeStruct((t, width), F32),
                   jax.ShapeDtypeStruct((t, width), BF16),
                   jax.ShapeDtypeStruct((t, width), BF16),
                   jax.ShapeDtypeStruct((width, t), BF16),
                   jax.ShapeDtypeStruct((t, width), BF16),
                   jax.ShapeDtypeStruct((t // GDN_CHUNK, 2 * SUBLANES, LANES), F32)],
        compiler_params=_params("parallel"),
        name="gdn_prep",
    )(qkv, qkv, ab, conv_w, alog_row, dtb_row)


def _gdn_scan_kernel(u_ref, w_ref, qh_ref, kt_ref, qk_ref, eg_ref, z_ref, on_ref, o_ref, s_ref):
    c = GDN_CHUNK
    nchunk = u_ref.shape[0] // c

    @pl.when(pl.program_id(1) == 0)
    def _():
        s_ref[...] = jnp.zeros_like(s_ref)

    gain = on_ref[...]
    for ci in range(nchunk):
        r0 = ci * c
        eg = eg_ref[ci]
        rows = slice(r0, r0 + c)
        cols = [slice(h * HEAD_DIM, (h + 1) * HEAD_DIM) for h in range(GDN_HEADS)]
        states = [s_ref[h] for h in range(GDN_HEADS)]
        sbs = [s.astype(BF16) for s in states]
        ws = [_dot(w_ref[rows, cs], sb) for cs, sb in zip(cols, sbs)]
        vbs = [(u_ref[rows, cs] - ws_h).astype(BF16) for cs, ws_h in zip(cols, ws)]
        for h, (cs, state, vb) in enumerate(zip(cols, states, vbs)):
            s_ref[h] = state * eg[h:h + 1, :] + _dot(kt_ref[cs, rows], vb)
        outs = [_dot(qh_ref[rows, cs], sb) + _dot(qk_ref[rows, cs], vb)
                for cs, sb, vb in zip(cols, sbs, vbs)]
        for cs, out in zip(cols, outs):
            zz = z_ref[rows, cs]
            o_ref[rows, cs] = (_rms(out, gain) * (zz * jax.nn.sigmoid(zz))).astype(o_ref.dtype)


def gdn_scan(u, w, qh, kt, qk, eg, z, out_norm, batch, tl):
    t, width = u.shape
    nblk = t // batch // tl
    tok = lambda b, i: (b * nblk + i, 0)
    return pl.pallas_call(
        _gdn_scan_kernel,
        grid=(batch, nblk),
        in_specs=[pl.BlockSpec((tl, width), tok),
                  pl.BlockSpec((tl, width), tok),
                  pl.BlockSpec((tl, width), tok),
                  pl.BlockSpec((width, tl), lambda b, i: (0, b * nblk + i)),
                  pl.BlockSpec((tl, width), tok),
                  pl.BlockSpec((tl // GDN_CHUNK, 2 * SUBLANES, LANES), lambda b, i: (b * nblk + i, 0, 0)),
                  pl.BlockSpec((tl, width), tok),
                  pl.BlockSpec((1, HEAD_DIM), lambda b, i: (0, 0))],
        out_specs=pl.BlockSpec((tl, width), tok),
        out_shape=jax.ShapeDtypeStruct((t, width), BF16),
        scratch_shapes=[pltpu.VMEM((GDN_HEADS, HEAD_DIM, HEAD_DIM), F32)],
        compiler_params=_params("parallel", "arbitrary"),
        name="gdn_scan",
    )(u, w, qh, kt, qk, eg, z, out_norm.reshape(1, HEAD_DIM))


def _forget_cumsum_kernel(f_ref, b_ref, k_ref, cfm_ref, kp_ref, carry_ref):
    @pl.when(pl.program_id(1) == 0)
    def _():
        carry_ref[...] = jnp.zeros_like(carry_ref)

    n = f_ref.shape[0]
    pre = f_ref[...] + b_ref[...]
    logf = jnp.minimum(pre, 0.0) - jnp.log(1.0 + jnp.exp(-jnp.abs(pre)))
    row = lax.broadcasted_iota(jnp.int32, (n, n), 0)
    col = lax.broadcasted_iota(jnp.int32, (n, n), 1)
    cs = _tri_cumsum((row >= col).astype(BF16), logf) + carry_ref[...]
    carry_ref[...] = cs[n - 1:n, :]
    cfm_ref[0] = cs.T[:SUBLANES, :]
    hi, mid, lo = (p.astype(F32) for p in _split3(-LOG2E * cs))
    lane = lax.broadcasted_iota(jnp.int32, (n, LANES), 1)
    for h in range(FOX_HEADS):
        aug = jnp.where(lane < 3, 1.0,
                        jnp.where(lane == 3, hi[:, h:h + 1],
                                  jnp.where(lane == 4, mid[:, h:h + 1],
                                            jnp.where(lane == 5, lo[:, h:h + 1], 0.0))))
        kp_ref[h, :, :HEAD_DIM] = k_ref[:, h * HEAD_DIM:(h + 1) * HEAD_DIM]
        kp_ref[h, :, HEAD_DIM:] = aug[:, :FOX_AUG].astype(BF16)


def forget_cumsum(flog, bias_row, k, batch, tl):
    t, width = k.shape
    seq = t // batch
    nblk = seq // tl
    tok = lambda b, i: (b * nblk + i, 0)
    return pl.pallas_call(
        _forget_cumsum_kernel,
        grid=(batch, nblk),
        in_specs=[pl.BlockSpec((tl, LANES), tok),
                  pl.BlockSpec((1, LANES), lambda b, i: (0, 0)),
                  pl.BlockSpec((tl, width), tok)],
        out_specs=[pl.BlockSpec((1, SUBLANES, tl), lambda b, i: (b, 0, i)),
                   pl.BlockSpec((FOX_HEADS, tl, HEAD_DIM + FOX_AUG), lambda b, i: (0, b * nblk + i, 0))],
        out_shape=[jax.ShapeDtypeStruct((batch, SUBLANES, seq), F32),
                   jax.ShapeDtypeStruct((FOX_HEADS, t, HEAD_DIM + FOX_AUG), BF16)],
        scratch_shapes=[pltpu.VMEM((1, LANES), F32)],
        compiler_params=_params("parallel", "arbitrary"),
        name="forget_cumsum",
    )(flog, bias_row, k)


def _fox_in_kernel(x_ref, g_ref, w_ref, cfm_ref, qp_ref, qm_ref):
    xn = _rms(x_ref[...], g_ref[...]).astype(BF16)
    qm_ref[...] = _dot(xn, w_ref[:, FOX_WIDTH:]).astype(qm_ref.dtype)
    qt = (_dot(xn, w_ref[:, :FOX_WIDTH]) * (LOG2E * HEAD_DIM ** -0.5)).T
    tm = qt.shape[1]
    hi, mid, lo = (p.astype(F32) for p in _split3(LOG2E * cfm_ref[0]))
    r = lax.broadcasted_iota(jnp.int32, (FOX_AUG, tm), 0)
    for h in range(FOX_HEADS):
        aug = jnp.where(r == 0, hi[h:h + 1, :],
                        jnp.where(r == 1, mid[h:h + 1, :],
                                  jnp.where(r == 2, lo[h:h + 1, :],
                                            jnp.where(r < 6, 1.0, 0.0))))
        qp_ref[h, :HEAD_DIM, :] = qt[h * HEAD_DIM:(h + 1) * HEAD_DIM, :].astype(BF16)
        qp_ref[h, HEAD_DIM:, :] = aug.astype(BF16)


def fox_in(x, gain, w, cfm, batch, tm):
    t, d = x.shape
    per_batch = t // batch // tm
    return pl.pallas_call(
        _fox_in_kernel,
        grid=(t // tm,),
        in_specs=[pl.BlockSpec((tm, d), lambda i: (i, 0)),
                  pl.BlockSpec((1, d), lambda i: (0, 0)),
                  pl.BlockSpec(w.shape, lambda i: (0, 0)),
                  pl.BlockSpec((1, SUBLANES, tm), lambda i: (i // per_batch, 0, i % per_batch))],
        out_specs=[pl.BlockSpec((FOX_HEADS, HEAD_DIM + FOX_AUG, tm), lambda i: (0, 0, i)),
                   pl.BlockSpec((tm, MEM_WIDTH), lambda i: (i, 0))],
        out_shape=[jax.ShapeDtypeStruct((FOX_HEADS, HEAD_DIM + FOX_AUG, t), BF16),
                   jax.ShapeDtypeStruct((t, MEM_WIDTH), BF16)],
        compiler_params=_params("parallel"),
        name="fox_in",
    )(x, gain.reshape(1, d), w, cfm)


def _fox_kernel(it_ref, jt_ref, qp_ref, kp_ref, vt_ref, o_ref, m_ref, acc_ref, *, tq, tk):
    step = pl.program_id(1)
    i = it_ref[step]
    j = jt_ref[step]

    @pl.when(j == 0)
    def _():
        m_ref[...] = jnp.full_like(m_ref, NEG_INF)
        acc_ref[...] = jnp.zeros_like(acc_ref)

    ones_rows = (lax.broadcasted_iota(jnp.int32, (FOX_AUG, tk), 0) == 0).astype(BF16)

    def heads(masked):
        if masked:
            causal = (lax.broadcasted_iota(jnp.int32, (tk, tq), 0)
                      <= lax.broadcasted_iota(jnp.int32, (tk, tq), 1))
        s_next = _dot(kp_ref[0], qp_ref[0])
        for h in range(FOX_HEADS):
            s = s_next
            if h + 1 < FOX_HEADS:
                s_next = _dot(kp_ref[h + 1], qp_ref[h + 1])
            if masked:
                s = jnp.where(causal, s, NEG_INF)
            m_prev = m_ref[h:h + 1, :]
            m_next = jnp.maximum(m_prev, jnp.max(s, axis=0, keepdims=True))
            alpha = jnp.exp2(m_prev - m_next)
            p = jnp.exp2(s - m_next).astype(BF16)
            vt = jnp.concatenate([vt_ref[h * HEAD_DIM:(h + 1) * HEAD_DIM, :], ones_rows], axis=0)
            acc_ref[h] = alpha * acc_ref[h] + _dot(vt, p)
            m_ref[h:h + 1, :] = m_next

    @pl.when(j < i)
    def _():
        heads(False)

    @pl.when(j == i)
    def _():
        heads(True)
        for h in range(FOX_HEADS):
            acc = acc_ref[h]
            out = acc[:HEAD_DIM, :] / acc[HEAD_DIM:HEAD_DIM + 1, :]
            o_ref[:, h * HEAD_DIM:(h + 1) * HEAD_DIM] = out.T.astype(o_ref.dtype)


def fox_attention(qp, kp, vt, batch, tq):
    heads, kdim, t = qp.shape
    width = vt.shape[0]
    seq = t // batch
    nq = seq // tq
    tk = tq
    pairs = [(i, j) for i in range(nq) for j in range(i + 1)]
    i_tab = jnp.asarray([p[0] for p in pairs], jnp.int32)
    j_tab = jnp.asarray([p[1] for p in pairs], jnp.int32)
    grid_spec = pltpu.PrefetchScalarGridSpec(
        num_scalar_prefetch=2,
        grid=(batch, len(pairs)),
        in_specs=[pl.BlockSpec((heads, kdim, tq), lambda b, s, it, jt: (0, 0, b * nq + it[s])),
                  pl.BlockSpec((heads, tk, kdim), lambda b, s, it, jt: (0, b * nq + jt[s], 0)),
                  pl.BlockSpec((width, tk), lambda b, s, it, jt: (0, b * nq + jt[s]))],
        out_specs=pl.BlockSpec((tq, width), lambda b, s, it, jt: (b * nq + it[s], 0)),
        scratch_shapes=[pltpu.VMEM((SUBLANES, tq), F32),
                        pltpu.VMEM((heads, kdim, tq), F32)])
    return pl.pallas_call(
        functools.partial(_fox_kernel, tq=tq, tk=tk),
        grid_spec=grid_spec,
        out_shape=jax.ShapeDtypeStruct((t, width), BF16),
        compiler_params=_params("parallel", "arbitrary"),
        name="fox_attention",
    )(i_tab, j_tab, qp, kp, vt)


def _mix_out_kernel(h_ref, main_ref, qm_ref, kbd_ref, vbd_ref, wo_ref, o_ref, *, main_width, mem_len):
    s = _dot(qm_ref[...], kbd_ref[0]) * (MEM_HEAD_DIM ** -0.5)
    probs = []
    for g in range(MEM_HEADS):
        sg = s[:, g * mem_len:(g + 1) * mem_len]
        p = jnp.exp(sg - jnp.max(sg, axis=1, keepdims=True))
        probs.append((p / jnp.sum(p, axis=1, keepdims=True)).astype(BF16))
    mem_out = _dot(jnp.concatenate(probs, axis=1), vbd_ref[0])
    o_ref[...] = (h_ref[...] + _dot(main_ref[...], wo_ref[:main_width, :])
                  + _dot(mem_out.astype(BF16), wo_ref[main_width:, :]))


def mix_out(h, main, qmem, kbd, vbd, wo, batch, tm):
    t, d = h.shape
    main_width = main.shape[1]
    mem_len = kbd.shape[2] // MEM_HEADS
    per_batch = t // batch // tm
    tok = lambda i: (i, 0)
    return pl.pallas_call(
        functools.partial(_mix_out_kernel, main_width=main_width, mem_len=mem_len),
        grid=(t // tm,),
        in_specs=[pl.BlockSpec((tm, d), tok),
                  pl.BlockSpec((tm, main_width), tok),
                  pl.BlockSpec((tm, MEM_WIDTH), tok),
                  pl.BlockSpec((1,) + kbd.shape[1:], lambda i: (i // per_batch, 0, 0)),
                  pl.BlockSpec((1,) + vbd.shape[1:], lambda i: (i // per_batch, 0, 0)),
                  pl.BlockSpec(wo.shape, lambda i: (0, 0))],
        out_specs=pl.BlockSpec((tm, d), tok),
        out_shape=jax.ShapeDtypeStruct((t, d), F32),
        compiler_params=_params("parallel"),
        name="mix_out",
    )(h, main, qmem, kbd, vbd, wo)


def _pad_cols(w, n):
    return jnp.pad(w, ((0, 0), (0, n - w.shape[1])))


def _lane_row(vec, offset=0):
    return jnp.zeros((1, LANES), F32).at[0, offset:offset + vec.shape[0]].set(vec.astype(F32))


def _memory_block_diag(kv, batch, mem_len):
    kv = kv.reshape(batch, mem_len, 2 * MEM_WIDTH)
    k, v = kv[..., :MEM_WIDTH], kv[..., MEM_WIDTH:]
    head_of_feat = jnp.arange(MEM_WIDTH) // MEM_HEAD_DIM
    head_of_slot = jnp.arange(MEM_HEADS * mem_len) // mem_len
    mask = head_of_feat[:, None] == head_of_slot[None, :]
    kbd = jnp.where(mask[None], jnp.tile(k.transpose(0, 2, 1), (1, 1, MEM_HEADS)), 0).astype(BF16)
    vbd = jnp.where(mask.T[None], jnp.tile(v, (1, MEM_HEADS, 1)), 0).astype(BF16)
    return kbd, vbd


def kernel(x, mem, ffn1_norm, ffn1_w_gate_up, ffn1_w_down, mix_norm, ffn2_norm, ffn2_w_gate_up,
           ffn2_w_down, gdn_w_in, gdn_conv, gdn_A_log, gdn_dt_bias, gdn_out_norm, fox_w_in, w_out,
           mem_norm, mem_w_kv, kv_norm, kv_w, kv_b_f, final_norm):
    bsz, seq, d = x.shape
    mem_len = mem.shape[1]
    depth = ffn1_norm.shape[0]
    n_a = gdn_w_in.shape[0]
    t = bsz * seq
    tm = min(512, seq)
    tf = 256
    tl = min(256, seq)
    tq = min(512, seq)

    h = x.reshape(t, d).astype(F32)

    w_kv_all = jnp.concatenate([mem_w_kv[l] for l in range(depth)], axis=1).astype(BF16)
    (kv_mem,) = norm_matmul(mem.reshape(bsz * mem_len, d).astype(F32), mem_norm, w_kv_all,
                            [(0, w_kv_all.shape[1], False)], [F32], min(256, bsz * mem_len), "mem_kv")

    o0, o1 = 3 * GDN_WIDTH, 4 * GDN_WIDTH
    kp = vt = cfm = None
    for l in range(depth):
        last = l == depth - 1
        h = ffn(h, ffn1_norm[l], ffn1_w_gate_up[l].astype(BF16), ffn1_w_down[l].astype(BF16),
                final_norm, tm, tf, False, "ffn1")
        kbd, vbd = _memory_block_diag(kv_mem[:, 2 * MEM_WIDTH * l:2 * MEM_WIDTH * (l + 1)], bsz, mem_len)
        if l < n_a:
            w_in = gdn_w_in[l]
            w_cat = jnp.concatenate([w_in[:, :o1], _pad_cols(w_in[:, o1:o1 + 2 * GDN_HEADS], LANES),
                                     w_in[:, o1 + 2 * GDN_HEADS:]], axis=1).astype(BF16)
            qkv, z, ab, qmem = norm_matmul(
                h, mix_norm[l], w_cat,
                [(0, o0, False), (o0, o1, False), (o1, o1 + LANES, False),
                 (o1 + LANES, o1 + LANES + MEM_WIDTH, False)],
                [F32, F32, F32, BF16], tm, "gdn_in")
            alog_row = _lane_row(gdn_A_log[l])
            dtb_row = _lane_row(gdn_dt_bias[l])
            u, w, qh, ktl, qk, eg = gdn_prep(qkv, ab, gdn_conv[l].astype(F32), alog_row, dtb_row, seq, tl)
            main = gdn_scan(u, w, qh, ktl, qk, eg, z, gdn_out_norm[l].astype(F32), bsz, tl)
        else:
            qp, qmem = fox_in(h, mix_norm[l], fox_w_in[l - n_a].astype(BF16), cfm, bsz, tm)
            main = fox_attention(qp, kp, vt, bsz, tq)
        h = mix_out(h, main, qmem, kbd, vbd, w_out[l].astype(BF16), bsz, tm)
        h = ffn(h, ffn2_norm[l], ffn2_w_gate_up[l].astype(BF16), ffn2_w_down[l].astype(BF16),
                final_norm, tm, tf, last, "ffn2")
        if l == n_a - 1:
            w_cat = jnp.concatenate([kv_w[:, :2 * FOX_WIDTH], _pad_cols(kv_w[:, 2 * FOX_WIDTH:], LANES)],
                                    axis=1).astype(BF16)
            k_sh, vt, flog = norm_matmul(
                h, kv_norm, w_cat,
                [(0, FOX_WIDTH, False), (FOX_WIDTH, 2 * FOX_WIDTH, True),
                 (2 * FOX_WIDTH, 2 * FOX_WIDTH + LANES, False)],
                [BF16, BF16, F32], tm, "shared_kv")
            cfm, kp = forget_cumsum(flog, _lane_row(kv_b_f), k_sh, bsz, tm)
    return h.reshape(bsz, seq, d)
```

```python
import functools

import jax
import jax.numpy as jnp
from jax import lax
from jax.experimental import pallas as pl
from jax.experimental.pallas import tpu as pltpu

F32 = jnp.float32
BF16 = jnp.bfloat16

EPS = 1e-6
NEG_INF = -1e30
HEAD_DIM = 128
GDN_HEADS = 6
GDN_WIDTH = GDN_HEADS * HEAD_DIM
FOX_HEADS = 6
FOX_WIDTH = FOX_HEADS * HEAD_DIM
CONV_WIDTH = 4
MEM_HEADS = 4
MEM_HEAD_DIM = 64
MEM_WIDTH = MEM_HEADS * MEM_HEAD_DIM

LANES = 128
SUBLANES = 8
GDN_CHUNK = 128
FOX_AUG = 16
LOG2E = 1.4426950408889634
FOX_SKIP_LOG2 = 160.0
VMEM_LIMIT = 56 * 1024 * 1024


def _params(*sem):
    return pltpu.CompilerParams(dimension_semantics=sem, vmem_limit_bytes=VMEM_LIMIT)


def _dot(a, b):
    return jnp.dot(a, b, preferred_element_type=F32)


def _dot_nt(a, b):
    return lax.dot_general(a, b, (((1,), (1,)), ((), ())), preferred_element_type=F32)


def _rms(x, gain):
    return x * lax.rsqrt(jnp.mean(x * x, axis=-1, keepdims=True) + EPS) * gain


def _split3(x):
    hi = x.astype(BF16)
    r1 = x - hi.astype(F32)
    mid = r1.astype(BF16)
    lo = (r1 - mid.astype(F32)).astype(BF16)
    return hi, mid, lo


def _tri_cumsum(tri, x):
    hi, mid, lo = _split3(x)
    return _dot(tri, hi) + _dot(tri, mid) + _dot(tri, lo)


def _norm_matmul_kernel(x_ref, g_ref, w_ref, *out_refs, splits):
    xn = _rms(x_ref[...], g_ref[...]).astype(BF16)
    for (lo, hi, transposed), o_ref in zip(splits, out_refs):
        y = _dot(xn, w_ref[:, lo:hi])
        if transposed:
            o_ref[...] = y.T.astype(o_ref.dtype)
        else:
            o_ref[...] = y.astype(o_ref.dtype)


def norm_matmul(x, gain, w, splits, dtypes, tm, name):
    t, d = x.shape
    n = w.shape[1]
    out_shape, out_specs = [], []
    for (lo, hi, transposed), dt in zip(splits, dtypes):
        if transposed:
            out_shape.append(jax.ShapeDtypeStruct((hi - lo, t), dt))
            out_specs.append(pl.BlockSpec((hi - lo, tm), lambda i: (0, i)))
        else:
            out_shape.append(jax.ShapeDtypeStruct((t, hi - lo), dt))
            out_specs.append(pl.BlockSpec((tm, hi - lo), lambda i: (i, 0)))
    return pl.pallas_call(
        functools.partial(_norm_matmul_kernel, splits=tuple(splits)),
        grid=(t // tm,),
        in_specs=[pl.BlockSpec((tm, d), lambda i: (i, 0)),
                  pl.BlockSpec((1, d), lambda i: (0, 0)),
                  pl.BlockSpec((d, n), lambda i: (0, 0))],
        out_specs=out_specs,
        out_shape=out_shape,
        compiler_params=_params("parallel"),
        name=name,
    )(x, gain.reshape(1, d), w)


def _ffn_kernel(h_ref, g_ref, wgu_ref, wd_ref, fg_ref, o_ref, act_ref, *, hidden, tf, final_norm):
    x = h_ref[...]
    xn = _rms(x, g_ref[...]).astype(BF16)
    for j in range(hidden // tf):
        gate = _dot(xn, wgu_ref[:, j * tf:(j + 1) * tf])
        up = _dot(xn, wgu_ref[:, hidden + j * tf:hidden + (j + 1) * tf])
        act_ref[:, j * tf:(j + 1) * tf] = (gate * jax.nn.sigmoid(gate) * up).astype(BF16)
    out = x + 0.5 * _dot(act_ref[...], wd_ref[...])
    if final_norm:
        out = _rms(out, fg_ref[...])
    o_ref[...] = out


def ffn(h, gain, wgu, wd, layer, final_gain, tm, tf, final_norm, name):
    t, d = h.shape
    hidden = wd.shape[1]
    return pl.pallas_call(
        functools.partial(_ffn_kernel, hidden=hidden, tf=tf, final_norm=final_norm),
        grid=(t // tm,),
        in_specs=[pl.BlockSpec((tm, d), lambda i: (i, 0)),
                  pl.BlockSpec((1, d), lambda i: (0, 0)),
                  pl.BlockSpec((None, d, 2 * hidden), lambda i: (layer, 0, 0), pipeline_mode=pl.Buffered(1)),
                  pl.BlockSpec((None, hidden, d), lambda i: (layer, 0, 0), pipeline_mode=pl.Buffered(1)),
                  pl.BlockSpec((1, d), lambda i: (0, 0))],
        out_specs=pl.BlockSpec((tm, d), lambda i: (i, 0)),
        out_shape=jax.ShapeDtypeStruct((t, d), F32),
        scratch_shapes=[pltpu.VMEM((tm, hidden), BF16)],
        compiler_params=_params("parallel"),
        name=name,
    )(h, gain.reshape(1, d), wgu, wd, final_gain.reshape(1, d))


def _unit_lower_inverses(mats):
    n = mats[0].shape[0]
    row = lax.broadcasted_iota(jnp.int32, (n, n), 0)
    col = lax.broadcasted_iota(jnp.int32, (n, n), 1)
    eye = (row == col).astype(F32)
    pair = (row >> 1) == (col >> 1)
    ts = [eye - jnp.where(pair, a, 0.0) for a in mats]
    s = 2
    while s < n:
        k = s.bit_length() - 1
        sub = ((row >> (k + 1)) == (col >> (k + 1))) & ((row >> k) != (col >> k))
        tbs = [t.astype(BF16) for t in ts]
        tes = [_dot(tb, jnp.where(sub, a, 0.0).astype(BF16)).astype(BF16) for tb, a in zip(tbs, mats)]
        ts = [t - _dot(te, tb) for t, te, tb in zip(ts, tes, tbs)]
        s *= 2
    return ts


def _gdn_prep_kernel(x_ref, xp_ref, ab_ref, cw_ref, alog_ref, dtb_ref,
                     u_ref, w_ref, qh_ref, kt_ref, qk_ref, eg_ref, *, blocks_per_seq):
    c = GDN_CHUNK
    nchunk = x_ref.shape[0] // c
    i = pl.program_id(0)
    first = (i % blocks_per_seq) == 0
    x = x_ref[...]
    prev = jnp.where(first, 0.0, xp_ref[...])
    xs = jnp.concatenate([prev, x], axis=0)
    ln = x.shape[0]
    y = cw_ref[CONV_WIDTH - 1:CONV_WIDTH, :] * x
    for tap in range(CONV_WIDTH - 1):
        shift = CONV_WIDTH - 1 - tap
        y = y + cw_ref[tap:tap + 1, :] * xs[SUBLANES - shift:SUBLANES - shift + ln, :]
    y = y * jax.nn.sigmoid(y)

    ab = ab_ref[...]
    pre = ab + dtb_ref[...]
    softplus = jnp.maximum(pre, 0.0) + jnp.log(1.0 + jnp.exp(-jnp.abs(pre)))
    g_col = -jnp.exp(alog_ref[...]) * softplus
    beta_col = jax.nn.sigmoid(ab)

    row = lax.broadcasted_iota(jnp.int32, (c, c), 0)
    col = lax.broadcasted_iota(jnp.int32, (c, c), 1)
    lower = row >= col
    tri = lower.astype(BF16)

    mats, rhss, where = [], [], []
    for ci in range(nchunk):
        r0 = ci * c
        gc_col = _tri_cumsum(tri, g_col[r0:r0 + c, :])
        gc_row = gc_col.T
        eg_ref[ci] = jnp.exp(jnp.broadcast_to(gc_row[0:2 * SUBLANES, c - 1:c], (2 * SUBLANES, LANES)))
        for h in range(GDN_HEADS):
            q = y[r0:r0 + c, h * HEAD_DIM:(h + 1) * HEAD_DIM]
            k = y[r0:r0 + c, GDN_WIDTH + h * HEAD_DIM:GDN_WIDTH + (h + 1) * HEAD_DIM]
            v = y[r0:r0 + c, 2 * GDN_WIDTH + h * HEAD_DIM:2 * GDN_WIDTH + (h + 1) * HEAD_DIM]
            q = q * lax.rsqrt(jnp.sum(q * q, axis=-1, keepdims=True) + EPS) * (HEAD_DIM ** -0.5)
            k = k * lax.rsqrt(jnp.sum(k * k, axis=-1, keepdims=True) + EPS)
            beta = beta_col[r0:r0 + c, GDN_HEADS + h:GDN_HEADS + h + 1]
            gcl = gc_col[:, h:h + 1]
            gr = gc_row[h:h + 1, :]
            glast = gr[:, c - 1:c]
            decay = jnp.where(lower, jnp.exp(jnp.where(lower, gcl - gr, 0.0)), 0.0)
            kb = k * beta
            kbf = k.astype(BF16)
            mats.append(jnp.where(row > col, _dot_nt(kb.astype(BF16), kbf) * decay, 0.0))
            qk = _dot_nt(q.astype(BF16), kbf) * decay
            egc = jnp.exp(gcl)
            rhss.append(jnp.concatenate([v * beta, kb * egc], axis=1).astype(BF16))
            cs = slice(h * HEAD_DIM, (h + 1) * HEAD_DIM)
            where.append((r0, cs))
            qh_ref[r0:r0 + c, cs] = (q * egc).astype(BF16)
            qk_ref[r0:r0 + c, cs] = qk.astype(BF16)
            kt_ref[cs, r0:r0 + c] = (k * jnp.exp(glast - gcl)).T.astype(BF16)

    for t, rhs, (r0, cs) in zip(_unit_lower_inverses(mats), rhss, where):
        sol = _dot(t.astype(BF16), rhs)
        u_ref[r0:r0 + c, cs] = sol[:, :HEAD_DIM]
        w_ref[r0:r0 + c, cs] = sol[:, HEAD_DIM:].astype(BF16)


def gdn_prep(qkv, ab, conv_w, alog_row, dtb_row, seq, tl):
    t = qkv.shape[0]
    w3 = qkv.shape[1]
    width = w3 // 3
    nblk = t // tl
    prev_blocks = tl // SUBLANES
    tok = lambda i: (i, 0)
    const = lambda i: (0, 0)
    return pl.pallas_call(
        functools.partial(_gdn_prep_kernel, blocks_per_seq=seq // tl),
        grid=(nblk,),
        in_specs=[pl.BlockSpec((tl, w3), tok),
                  pl.BlockSpec((SUBLANES, w3), lambda i: (jnp.maximum(i * prev_blocks - 1, 0), 0)),
                  pl.BlockSpec((tl, LANES), tok),
                  pl.BlockSpec((CONV_WIDTH, w3), const),
                  pl.BlockSpec((1, LANES), const),
                  pl.BlockSpec((1, LANES), const)],
        out_specs=[pl.BlockSpec((tl, width), tok),
                   pl.BlockSpec((tl, width), tok),
                   pl.BlockSpec((tl, width), tok),
                   pl.BlockSpec((width, tl), lambda i: (0, i)),
                   pl.BlockSpec((tl, width), tok),
                   pl.BlockSpec((tl // GDN_CHUNK, 2 * SUBLANES, LANES), lambda i: (i, 0, 0))],
        out_shape=[jax.ShapeDtypeStruct((t, width), F32),
                   jax.ShapeDtypeStruct((t, width), BF16),
                   jax.ShapeDtypeStruct((t, width), BF16),
                   jax.ShapeDtypeStruct((width, t), BF16),
                   jax.ShapeDtypeStruct((t, width), BF16),
                   jax.ShapeDtypeStruct((t // GDN_CHUNK, 2 * SUBLANES, LANES), F32)],
        compiler_params=_params("parallel"),
        name="gdn_prep",
    )(qkv, qkv, ab, conv_w, alog_row, dtb_row)


def _gdn_scan_kernel(u_ref, w_ref, qh_ref, kt_ref, qk_ref, eg_ref, z_ref, on_ref, o_ref, s_ref):
    c = GDN_CHUNK
    nchunk = u_ref.shape[0] // c

    @pl.when(pl.program_id(1) == 0)
    def _():
        s_ref[...] = jnp.zeros_like(s_ref)

    gain = on_ref[...]
    for ci in range(nchunk):
        r0 = ci * c
        eg = eg_ref[ci]
        rows = slice(r0, r0 + c)
        cols = [slice(h * HEAD_DIM, (h + 1) * HEAD_DIM) for h in range(GDN_HEADS)]
        states = [s_ref[h] for h in range(GDN_HEADS)]
        sbs = [s.astype(BF16) for s in states]
        ws = [_dot(w_ref[rows, cs], sb) for cs, sb in zip(cols, sbs)]
        vbs = [(u_ref[rows, cs] - ws_h).astype(BF16) for cs, ws_h in zip(cols, ws)]
        for h, (cs, state, vb) in enumerate(zip(cols, states, vbs)):
            s_ref[h] = state * eg[h:h + 1, :] + _dot(kt_ref[cs, rows], vb)
        outs = [_dot(qh_ref[rows, cs], sb) + _dot(qk_ref[rows, cs], vb)
                for cs, sb, vb in zip(cols, sbs, vbs)]
        for cs, out in zip(cols, outs):
            zz = z_ref[rows, cs]
            o_ref[rows, cs] = (_rms(out, gain) * (zz * jax.nn.sigmoid(zz))).astype(o_ref.dtype)


def gdn_scan(u, w, qh, kt, qk, eg, z, out_norm, batch, tl):
    t, width = u.shape
    nblk = t // batch // tl
    tok = lambda b, i: (b * nblk + i, 0)
    return pl.pallas_call(
        _gdn_scan_kernel,
        grid=(batch, nblk),
        in_specs=[pl.BlockSpec((tl, width), tok),
                  pl.BlockSpec((tl, width), tok),
                  pl.BlockSpec((tl, width), tok),
                  pl.BlockSpec((width, tl), lambda b, i: (0, b * nblk + i)),
                  pl.BlockSpec((tl, width), tok),
                  pl.BlockSpec((tl // GDN_CHUNK, 2 * SUBLANES, LANES), lambda b, i: (b * nblk + i, 0, 0)),
                  pl.BlockSpec((tl, width), tok),
                  pl.BlockSpec((1, HEAD_DIM), lambda b, i: (0, 0))],
        out_specs=pl.BlockSpec((tl, width), tok),
        out_shape=jax.ShapeDtypeStruct((t, width), BF16),
        scratch_shapes=[pltpu.VMEM((GDN_HEADS, HEAD_DIM, HEAD_DIM), F32)],
        compiler_params=_params("parallel", "arbitrary"),
        name="gdn_scan",
    )(u, w, qh, kt, qk, eg, z, out_norm.reshape(1, HEAD_DIM))


def _forget_cumsum_kernel(f_ref, b_ref, k_ref, cfm_ref, kp_ref, carry_ref):
    @pl.when(pl.program_id(1) == 0)
    def _():
        carry_ref[...] = jnp.zeros_like(carry_ref)

    n = f_ref.shape[0]
    pre = f_ref[...] + b_ref[...]
    logf = jnp.minimum(pre, 0.0) - jnp.log(1.0 + jnp.exp(-jnp.abs(pre)))
    row = lax.broadcasted_iota(jnp.int32, (n, n), 0)
    col = lax.broadcasted_iota(jnp.int32, (n, n), 1)
    cs = _tri_cumsum((row >= col).astype(BF16), logf) + carry_ref[...]
    carry_ref[...] = cs[n - 1:n, :]
    cfm_ref[0] = cs.T[:SUBLANES, :]
    hi, mid, lo = (p.astype(F32) for p in _split3(-LOG2E * cs))
    lane = lax.broadcasted_iota(jnp.int32, (n, LANES), 1)
    for h in range(FOX_HEADS):
        aug = jnp.where(lane < 3, 1.0,
                        jnp.where(lane == 3, hi[:, h:h + 1],
                                  jnp.where(lane == 4, mid[:, h:h + 1],
                                            jnp.where(lane == 5, lo[:, h:h + 1], 0.0))))
        kp_ref[h, :, :HEAD_DIM] = k_ref[:, h * HEAD_DIM:(h + 1) * HEAD_DIM]
        kp_ref[h, :, HEAD_DIM:] = aug[:, :FOX_AUG].astype(BF16)


def forget_cumsum(flog, bias_row, k, batch, tl):
    t, width = k.shape
    seq = t // batch
    nblk = seq // tl
    tok = lambda b, i: (b * nblk + i, 0)
    return pl.pallas_call(
        _forget_cumsum_kernel,
        grid=(batch, nblk),
        in_specs=[pl.BlockSpec((tl, LANES), tok),
                  pl.BlockSpec((1, LANES), lambda b, i: (0, 0)),
                  pl.BlockSpec((tl, width), tok)],
        out_specs=[pl.BlockSpec((1, SUBLANES, tl), lambda b, i: (b, 0, i)),
                   pl.BlockSpec((FOX_HEADS, tl, HEAD_DIM + FOX_AUG), lambda b, i: (0, b * nblk + i, 0))],
        out_shape=[jax.ShapeDtypeStruct((batch, SUBLANES, seq), F32),
                   jax.ShapeDtypeStruct((FOX_HEADS, t, HEAD_DIM + FOX_AUG), BF16)],
        scratch_shapes=[pltpu.VMEM((1, LANES), F32)],
        compiler_params=_params("parallel", "arbitrary"),
        name="forget_cumsum",
    )(flog, bias_row, k)


def _fox_in_kernel(x_ref, g_ref, w_ref, cfm_ref, qp_ref, qm_ref):
    xn = _rms(x_ref[...], g_ref[...]).astype(BF16)
    qm_ref[...] = _dot(xn, w_ref[:, FOX_WIDTH:]).astype(qm_ref.dtype)
    qt = (_dot(xn, w_ref[:, :FOX_WIDTH]) * (LOG2E * HEAD_DIM ** -0.5)).T
    tm = qt.shape[1]
    hi, mid, lo = (p.astype(F32) for p in _split3(LOG2E * cfm_ref[0]))
    r = lax.broadcasted_iota(jnp.int32, (FOX_AUG, tm), 0)
    for h in range(FOX_HEADS):
        aug = jnp.where(r == 0, hi[h:h + 1, :],
                        jnp.where(r == 1, mid[h:h + 1, :],
                                  jnp.where(r == 2, lo[h:h + 1, :],
                                            jnp.where(r < 6, 1.0, 0.0))))
        qp_ref[h, :HEAD_DIM, :] = qt[h * HEAD_DIM:(h + 1) * HEAD_DIM, :].astype(BF16)
        qp_ref[h, HEAD_DIM:, :] = aug.astype(BF16)


def fox_in(x, gain, w, cfm, batch, tm):
    t, d = x.shape
    per_batch = t // batch // tm
    return pl.pallas_call(
        _fox_in_kernel,
        grid=(t // tm,),
        in_specs=[pl.BlockSpec((tm, d), lambda i: (i, 0)),
                  pl.BlockSpec((1, d), lambda i: (0, 0)),
                  pl.BlockSpec(w.shape, lambda i: (0, 0)),
                  pl.BlockSpec((1, SUBLANES, tm), lambda i: (i // per_batch, 0, i % per_batch))],
        out_specs=[pl.BlockSpec((FOX_HEADS, HEAD_DIM + FOX_AUG, tm), lambda i: (0, 0, i)),
                   pl.BlockSpec((tm, MEM_WIDTH), lambda i: (i, 0))],
        out_shape=[jax.ShapeDtypeStruct((FOX_HEADS, HEAD_DIM + FOX_AUG, t), BF16),
                   jax.ShapeDtypeStruct((t, MEM_WIDTH), BF16)],
        compiler_params=_params("parallel"),
        name="fox_in",
    )(x, gain.reshape(1, d), w, cfm)


def _fox_kernel(it_ref, jt_ref, kind_ref, first_ref, qp_ref, kp_ref, vt_ref, o_ref, m_ref, acc_ref,
                *, tq, tk, nsteps):
    del it_ref, jt_ref
    step = pl.program_id(0) * nsteps + pl.program_id(1)
    kind = kind_ref[step]

    @pl.when(first_ref[step] == 1)
    def _():
        m_ref[...] = jnp.full_like(m_ref, NEG_INF)
        acc_ref[...] = jnp.zeros_like(acc_ref)

    ones_rows = (lax.broadcasted_iota(jnp.int32, (FOX_AUG, tk), 0) == 0).astype(BF16)

    def heads(masked):
        if masked:
            causal = (lax.broadcasted_iota(jnp.int32, (tk, tq), 0)
                      <= lax.broadcasted_iota(jnp.int32, (tk, tq), 1))
        s_next = _dot(kp_ref[0], qp_ref[0])
        for h in range(FOX_HEADS):
            s = s_next
            if h + 1 < FOX_HEADS:
                s_next = _dot(kp_ref[h + 1], qp_ref[h + 1])
            if masked:
                s = jnp.where(causal, s, NEG_INF)
            m_prev = m_ref[h:h + 1, :]
            m_next = jnp.maximum(m_prev, jnp.max(s, axis=0, keepdims=True))
            alpha = jnp.exp2(m_prev - m_next)
            p = jnp.exp2(s - m_next).astype(BF16)
            vt = jnp.concatenate([vt_ref[h * HEAD_DIM:(h + 1) * HEAD_DIM, :], ones_rows], axis=0)
            acc_ref[h] = alpha * acc_ref[h] + _dot(vt, p)
            m_ref[h:h + 1, :] = m_next

    @pl.when(kind == 1)
    def _():
        heads(False)

    @pl.when(kind == 2)
    def _():
        heads(True)
        for h in range(FOX_HEADS):
            acc = acc_ref[h]
            out = acc[:HEAD_DIM, :] / acc[HEAD_DIM:HEAD_DIM + 1, :]
            o_ref[:, h * HEAD_DIM:(h + 1) * HEAD_DIM] = out.T.astype(o_ref.dtype)


def _fox_bounds_kernel(qp_ref, kp_ref, cfm_ref, bnd_ref, cend_ref, kmax_ref):
    @pl.when(pl.program_id(1) == 0)
    def _():
        kmax_ref[...] = jnp.zeros_like(kmax_ref)

    tq = qp_ref.shape[2]
    c2 = LOG2E * cfm_ref[0]
    bnds, cends = [], []
    for h in range(FOX_HEADS):
        q = qp_ref[h, :HEAD_DIM, :].astype(F32)
        k = kp_ref[h, :, :HEAD_DIM].astype(F32)
        k_norm = jnp.sqrt(jnp.max(jnp.sum(k * k, axis=1, keepdims=True), axis=0, keepdims=True))
        kmax = jnp.maximum(kmax_ref[h:h + 1, :], k_norm)
        kmax_ref[h:h + 1, :] = kmax
        q_norm = jnp.sqrt(jnp.sum(q * q, axis=0, keepdims=True))
        diag = jnp.sum(q * k.T, axis=0, keepdims=True)
        c2h = c2[h:h + 1, :]
        worst = jnp.max(q_norm * kmax[:, :1] + c2h - diag, axis=1, keepdims=True)
        bnds.append(jnp.broadcast_to(worst + FOX_SKIP_LOG2, (1, LANES)))
        cends.append(jnp.broadcast_to(c2h[:, tq - 1:tq], (1, LANES)))
    pad = [jnp.zeros((SUBLANES - FOX_HEADS, LANES), F32)]
    bnd_ref[0, 0] = jnp.concatenate(bnds + pad, axis=0)
    cend_ref[0, 0] = jnp.concatenate(cends + pad, axis=0)


def fox_bounds(qp, kp, cfm, batch, tq):
    heads, kdim, t = qp.shape
    nq = t // batch // tq
    out = jax.ShapeDtypeStruct((batch, nq, SUBLANES, LANES), F32)
    return pl.pallas_call(
        _fox_bounds_kernel,
        grid=(batch, nq),
        in_specs=[pl.BlockSpec((heads, kdim, tq), lambda b, i: (0, 0, b * nq + i)),
                  pl.BlockSpec((heads, tq, kdim), lambda b, i: (0, b * nq + i, 0)),
                  pl.BlockSpec((1, SUBLANES, tq), lambda b, i: (b, 0, i))],
        out_specs=[pl.BlockSpec((1, 1, SUBLANES, LANES), lambda b, i: (b, i, 0, 0)),
                   pl.BlockSpec((1, 1, SUBLANES, LANES), lambda b, i: (b, i, 0, 0))],
        out_shape=[out, out],
        scratch_shapes=[pltpu.VMEM((SUBLANES, LANES), F32)],
        compiler_params=_params("parallel", "arbitrary"),
        name="fox_bounds",
    )(qp, kp, cfm)


def _fox_step_tables(bnd, cend, nq):
    nsteps = nq * (nq + 1) // 2
    blk = jnp.arange(nq, dtype=jnp.int32)
    bnd = bnd[:, :, :FOX_HEADS, 0]
    cend = cend[:, :, :FOX_HEADS, 0]
    negligible = jnp.all(cend[:, None, :, :] >= bnd[:, :, None, :], axis=-1)
    needed = (~negligible | (blk[None, :] >= blk[:, None])[None]).astype(jnp.int32)
    j_first = jnp.sum((jnp.cumsum(needed, axis=2) == 0).astype(jnp.int32), axis=2)
    count = blk[None, :] + 1 - j_first
    end = jnp.cumsum(count, axis=1)
    start = end - count
    slot = jnp.arange(nsteps, dtype=jnp.int32)[None, :, None]
    it = jnp.minimum(jnp.sum((slot >= end[:, None, :]).astype(jnp.int32), axis=2), nq - 1)
    mine = (it[:, :, None] == blk[None, None, :]).astype(jnp.int32)
    jt = jnp.minimum(jnp.sum(mine * (j_first - start)[:, None, :], axis=2) + slot[:, :, 0], it)
    valid = (slot[:, :, 0] < end[:, nq - 1:nq]).astype(jnp.int32)
    kind = valid * (1 + (jt == it).astype(jnp.int32))
    first = valid * jnp.sum(mine * (start[:, None, :] == slot).astype(jnp.int32), axis=2)
    return it.reshape(-1), jt.reshape(-1), kind.reshape(-1), first.reshape(-1), nsteps


def fox_attention(qp, kp, vt, cfm, batch, tq):
    heads, kdim, t = qp.shape
    width = vt.shape[0]
    seq = t // batch
    nq = seq // tq
    tk = tq
    bnd, cend = fox_bounds(qp, kp, cfm, batch, tq)
    i_tab, j_tab, kind, first, nsteps = _fox_step_tables(bnd, cend, nq)
    qblk = lambda b, s, it, jt, kd, fs: b * nq + it[b * nsteps + s]
    kblk = lambda b, s, it, jt, kd, fs: b * nq + jt[b * nsteps + s]
    grid_spec = pltpu.PrefetchScalarGridSpec(
        num_scalar_prefetch=4,
        grid=(batch, nsteps),
        in_specs=[pl.BlockSpec((heads, kdim, tq), lambda *a: (0, 0, qblk(*a))),
                  pl.BlockSpec((heads, tk, kdim), lambda *a: (0, kblk(*a), 0)),
                  pl.BlockSpec((width, tk), lambda *a: (0, kblk(*a)))],
        out_specs=pl.BlockSpec((tq, width), lambda *a: (qblk(*a), 0)),
        scratch_shapes=[pltpu.VMEM((SUBLANES, tq), F32),
                        pltpu.VMEM((heads, kdim, tq), F32)])
    return pl.pallas_call(
        functools.partial(_fox_kernel, tq=tq, tk=tk, nsteps=nsteps),
        grid_spec=grid_spec,
        out_shape=jax.ShapeDtypeStruct((t, width), BF16),
        compiler_params=_params("parallel", "arbitrary"),
        name="fox_attention",
    )(i_tab, j_tab, kind, first, qp, kp, vt)


def _mix_out_kernel(h_ref, main_ref, qm_ref, kbd_ref, vbd_ref, wo_ref, o_ref, *, main_width, mem_len):
    s = _dot(qm_ref[...], kbd_ref[0]) * (MEM_HEAD_DIM ** -0.5)
    probs = []
    for g in range(MEM_HEADS):
        sg = s[:, g * mem_len:(g + 1) * mem_len]
        p = jnp.exp(sg - jnp.max(sg, axis=1, keepdims=True))
        probs.append((p / jnp.sum(p, axis=1, keepdims=True)).astype(BF16))
    mem_out = _dot(jnp.concatenate(probs, axis=1), vbd_ref[0])
    o_ref[...] = (h_ref[...] + _dot(main_ref[...], wo_ref[:main_width, :])
                  + _dot(mem_out.astype(BF16), wo_ref[main_width:, :]))


def mix_out(h, main, qmem, kbd, vbd, wo, layer, batch, tm):
    t, d = h.shape
    main_width = main.shape[1]
    mem_len = kbd.shape[2] // MEM_HEADS
    per_batch = t // batch // tm
    tok = lambda i: (i, 0)
    return pl.pallas_call(
        functools.partial(_mix_out_kernel, main_width=main_width, mem_len=mem_len),
        grid=(t // tm,),
        in_specs=[pl.BlockSpec((tm, d), tok),
                  pl.BlockSpec((tm, main_width), tok),
                  pl.BlockSpec((tm, MEM_WIDTH), tok),
                  pl.BlockSpec((1,) + kbd.shape[1:], lambda i: (i // per_batch, 0, 0)),
                  pl.BlockSpec((1,) + vbd.shape[1:], lambda i: (i // per_batch, 0, 0)),
                  pl.BlockSpec((None,) + wo.shape[1:], lambda i: (layer, 0, 0))],
        out_specs=pl.BlockSpec((tm, d), tok),
        out_shape=jax.ShapeDtypeStruct((t, d), F32),
        compiler_params=_params("parallel"),
        name="mix_out",
    )(h, main, qmem, kbd, vbd, wo)


def _pad_cols(w, n):
    return jnp.pad(w, ((0, 0), (0, n - w.shape[1])))


def _lane_row(vec, offset=0):
    return jnp.zeros((1, LANES), F32).at[0, offset:offset + vec.shape[0]].set(vec.astype(F32))


def _memory_block_diag(kv, batch, mem_len):
    kv = kv.reshape(batch, mem_len, 2 * MEM_WIDTH)
    k, v = kv[..., :MEM_WIDTH], kv[..., MEM_WIDTH:]
    head_of_feat = jnp.arange(MEM_WIDTH) // MEM_HEAD_DIM
    head_of_slot = jnp.arange(MEM_HEADS * mem_len) // mem_len
    mask = head_of_feat[:, None] == head_of_slot[None, :]
    kbd = jnp.where(mask[None], jnp.tile(k.transpose(0, 2, 1), (1, 1, MEM_HEADS)), 0).astype(BF16)
    vbd = jnp.where(mask.T[None], jnp.tile(v, (1, MEM_HEADS, 1)), 0).astype(BF16)
    return kbd, vbd


def kernel(x, mem, ffn1_norm, ffn1_w_gate_up, ffn1_w_down, mix_norm, ffn2_norm, ffn2_w_gate_up,
           ffn2_w_down, gdn_w_in, gdn_conv, gdn_A_log, gdn_dt_bias, gdn_out_norm, fox_w_in, w_out,
           mem_norm, mem_w_kv, kv_norm, kv_w, kv_b_f, final_norm):
    bsz, seq, d = x.shape
    mem_len = mem.shape[1]
    depth = ffn1_norm.shape[0]
    n_a = gdn_w_in.shape[0]
    t = bsz * seq
    tm = min(512, seq)
    tf = 256
    tl = min(256, seq)
    tq = min(512, seq)

    h = x.reshape(t, d).astype(F32)

    w_kv_all = jnp.concatenate([mem_w_kv[l] for l in range(depth)], axis=1).astype(BF16)
    (kv_mem,) = norm_matmul(mem.reshape(bsz * mem_len, d).astype(F32), mem_norm, w_kv_all,
                            [(0, w_kv_all.shape[1], False)], [F32], min(256, bsz * mem_len), "mem_kv")

    o0, o1 = 3 * GDN_WIDTH, 4 * GDN_WIDTH
    kp = vt = cfm = None
    wgu1, wd1 = ffn1_w_gate_up.astype(BF16), ffn1_w_down.astype(BF16)
    wgu2, wd2 = ffn2_w_gate_up.astype(BF16), ffn2_w_down.astype(BF16)
    wo_all = w_out.astype(BF16)
    for l in range(depth):
        last = l == depth - 1
        h = ffn(h, ffn1_norm[l], wgu1, wd1, l, final_norm, tm, tf, False, "ffn1")
        kbd, vbd = _memory_block_diag(kv_mem[:, 2 * MEM_WIDTH * l:2 * MEM_WIDTH * (l + 1)], bsz, mem_len)
        if l < n_a:
            w_in = gdn_w_in[l]
            w_cat = jnp.concatenate([w_in[:, :o1], _pad_cols(w_in[:, o1:o1 + 2 * GDN_HEADS], LANES),
                                     w_in[:, o1 + 2 * GDN_HEADS:]], axis=1).astype(BF16)
            qkv, z, ab, qmem = norm_matmul(
                h, mix_norm[l], w_cat,
                [(0, o0, False), (o0, o1, False), (o1, o1 + LANES, False),
                 (o1 + LANES, o1 + LANES + MEM_WIDTH, False)],
                [F32, F32, F32, BF16], tm, "gdn_in")
            alog_row = _lane_row(gdn_A_log[l])
            dtb_row = _lane_row(gdn_dt_bias[l])
            u, w, qh, ktl, qk, eg = gdn_prep(qkv, ab, gdn_conv[l].astype(F32), alog_row, dtb_row, seq, tl)
            main = gdn_scan(u, w, qh, ktl, qk, eg, z, gdn_out_norm[l].astype(F32), bsz, tl)
        else:
            qp, qmem = fox_in(h, mix_norm[l], fox_w_in[l - n_a].astype(BF16), cfm, bsz, tm)
            main = fox_attention(qp, kp, vt, cfm, bsz, tq)
        h = mix_out(h, main, qmem, kbd, vbd, wo_all, l, bsz, tm)
        h = ffn(h, ffn2_norm[l], wgu2, wd2, l, final_norm, tm, tf, last, "ffn2")
        if l == n_a - 1:
            w_cat = jnp.concatenate([kv_w[:, :2 * FOX_WIDTH], _pad_cols(kv_w[:, 2 * FOX_WIDTH:], LANES)],
                                    axis=1).astype(BF16)
            k_sh, vt, flog = norm_matmul(
                h, kv_norm, w_cat,
                [(0, FOX_WIDTH, False), (FOX_WIDTH, 2 * FOX_WIDTH, True),
                 (2 * FOX_WIDTH, 2 * FOX_WIDTH + LANES, False)],
                [BF16, BF16, F32], tm, "shared_kv")
            cfm, kp = forget_cumsum(flog, _lane_row(kv_b_f), k_sh, bsz, tm)
    return h.reshape(bsz, seq, d)
```

```python
import functools

import jax
import jax.numpy as jnp
from jax import lax
from jax.experimental import pallas as pl
from jax.experimental.pallas import tpu as pltpu

F32 = jnp.float32
BF16 = jnp.bfloat16

EPS = 1e-6
NEG_INF = -1e30
HEAD_DIM = 128
GDN_HEADS = 6
GDN_WIDTH = GDN_HEADS * HEAD_DIM
FOX_HEADS = 6
FOX_WIDTH = FOX_HEADS * HEAD_DIM
CONV_WIDTH = 4
MEM_HEADS = 4
MEM_HEAD_DIM = 64
MEM_WIDTH = MEM_HEADS * MEM_HEAD_DIM

LANES = 128
SUBLANES = 8
GDN_CHUNK = 128
FOX_AUG = 16
LOG2E = 1.4426950408889634
FOX_SKIP_LOG2 = 160.0
VMEM_LIMIT = 56 * 1024 * 1024


def _params(*sem):
    return pltpu.CompilerParams(dimension_semantics=sem, vmem_limit_bytes=VMEM_LIMIT)


def _dot(a, b):
    return jnp.dot(a, b, preferred_element_type=F32)


def _dot_nt(a, b):
    return lax.dot_general(a, b, (((1,), (1,)), ((), ())), preferred_element_type=F32)


def _rms(x, gain):
    return x * lax.rsqrt(jnp.mean(x * x, axis=-1, keepdims=True) + EPS) * gain


def _split3(x):
    hi = x.astype(BF16)
    r1 = x - hi.astype(F32)
    mid = r1.astype(BF16)
    lo = (r1 - mid.astype(F32)).astype(BF16)
    return hi, mid, lo


def _tri_cumsum(tri, x):
    hi, mid, lo = _split3(x)
    return _dot(tri, hi) + _dot(tri, mid) + _dot(tri, lo)


def _norm_matmul_kernel(x_ref, g_ref, w_ref, *out_refs, splits):
    xn = _rms(x_ref[...], g_ref[...]).astype(BF16)
    for (lo, hi, transposed), o_ref in zip(splits, out_refs):
        y = _dot(xn, w_ref[:, lo:hi])
        if transposed:
            o_ref[...] = y.T.astype(o_ref.dtype)
        else:
            o_ref[...] = y.astype(o_ref.dtype)


def norm_matmul(x, gain, w, splits, dtypes, tm, name):
    t, d = x.shape
    n = w.shape[1]
    out_shape, out_specs = [], []
    for (lo, hi, transposed), dt in zip(splits, dtypes):
        if transposed:
            out_shape.append(jax.ShapeDtypeStruct((hi - lo, t), dt))
            out_specs.append(pl.BlockSpec((hi - lo, tm), lambda i: (0, i)))
        else:
            out_shape.append(jax.ShapeDtypeStruct((t, hi - lo), dt))
            out_specs.append(pl.BlockSpec((tm, hi - lo), lambda i: (i, 0)))
    return pl.pallas_call(
        functools.partial(_norm_matmul_kernel, splits=tuple(splits)),
        grid=(t // tm,),
        in_specs=[pl.BlockSpec((tm, d), lambda i: (i, 0)),
                  pl.BlockSpec((1, d), lambda i: (0, 0)),
                  pl.BlockSpec((d, n), lambda i: (0, 0))],
        out_specs=out_specs,
        out_shape=out_shape,
        compiler_params=_params("parallel"),
        name=name,
    )(x, gain.reshape(1, d), w)


def _ffn_kernel(h_ref, g_ref, wgu_ref, wd_ref, fg_ref, o_ref, act_ref, *, hidden, tf, final_norm):
    x = h_ref[...]
    xn = _rms(x, g_ref[...]).astype(BF16)
    for j in range(hidden // tf):
        gate = _dot(xn, wgu_ref[:, j * tf:(j + 1) * tf])
        up = _dot(xn, wgu_ref[:, hidden + j * tf:hidden + (j + 1) * tf])
        act_ref[:, j * tf:(j + 1) * tf] = (gate * jax.nn.sigmoid(gate) * up).astype(BF16)
    out = x + 0.5 * _dot(act_ref[...], wd_ref[...])
    if final_norm:
        out = _rms(out, fg_ref[...])
    o_ref[...] = out


def ffn(h, gain, wgu, wd, layer, final_gain, tm, tf, final_norm, name):
    t, d = h.shape
    hidden = wd.shape[1]
    return pl.pallas_call(
        functools.partial(_ffn_kernel, hidden=hidden, tf=tf, final_norm=final_norm),
        grid=(t // tm,),
        in_specs=[pl.BlockSpec((tm, d), lambda i: (i, 0)),
                  pl.BlockSpec((1, d), lambda i: (0, 0)),
                  pl.BlockSpec((None, d, 2 * hidden), lambda i: (layer, 0, 0), pipeline_mode=pl.Buffered(1)),
                  pl.BlockSpec((None, hidden, d), lambda i: (layer, 0, 0), pipeline_mode=pl.Buffered(1)),
                  pl.BlockSpec((1, d), lambda i: (0, 0))],
        out_specs=pl.BlockSpec((tm, d), lambda i: (i, 0)),
        out_shape=jax.ShapeDtypeStruct((t, d), F32),
        scratch_shapes=[pltpu.VMEM((tm, hidden), BF16)],
        compiler_params=_params("parallel"),
        name=name,
    )(h, gain.reshape(1, d), wgu, wd, final_gain.reshape(1, d))


def _unit_lower_inverses(mats):
    n = mats[0].shape[0]
    row = lax.broadcasted_iota(jnp.int32, (n, n), 0)
    col = lax.broadcasted_iota(jnp.int32, (n, n), 1)
    eye = (row == col).astype(F32)
    pair = (row >> 1) == (col >> 1)
    ts = [eye - jnp.where(pair, a, 0.0) for a in mats]
    s = 2
    while s < n:
        k = s.bit_length() - 1
        sub = ((row >> (k + 1)) == (col >> (k + 1))) & ((row >> k) != (col >> k))
        tbs = [t.astype(BF16) for t in ts]
        tes = [_dot(tb, jnp.where(sub, a, 0.0).astype(BF16)).astype(BF16) for tb, a in zip(tbs, mats)]
        ts = [t - _dot(te, tb) for t, te, tb in zip(ts, tes, tbs)]
        s *= 2
    return ts


def _gdn_in_kernel(x_ref, g_ref, w_ref, qkv_ref, z_ref, ab_ref, qm_ref, wb_ref):
    @pl.when(pl.program_id(0) == 0)
    def _():
        wb_ref[...] = w_ref[...].astype(BF16)

    o0, o1 = 3 * GDN_WIDTH, 4 * GDN_WIDTH
    ab_off = 2 * GDN_HEADS
    xn = _rms(x_ref[...], g_ref[...]).astype(BF16)
    qkv_ref[...] = _dot(xn, wb_ref[:, :o0])
    z_ref[...] = _dot(xn, wb_ref[:, o0:o1])
    rest = _dot(xn, wb_ref[:, o1:])
    ab_ref[...] = rest[:, :LANES]
    qm_ref[...] = rest[:, ab_off:ab_off + MEM_WIDTH].astype(qm_ref.dtype)


def gdn_in(h, gain, w_in, layer, tm):
    t, d = h.shape
    n = w_in.shape[2]
    w3 = 3 * GDN_WIDTH
    tok = lambda i: (i, 0)
    const = lambda i: (0, 0)
    return pl.pallas_call(
        _gdn_in_kernel,
        grid=(t // tm,),
        in_specs=[pl.BlockSpec((tm, d), tok),
                  pl.BlockSpec((1, d), const),
                  pl.BlockSpec((None, d, n), lambda i: (layer, 0, 0), pipeline_mode=pl.Buffered(1))],
        out_specs=[pl.BlockSpec((tm, w3), tok),
                   pl.BlockSpec((tm, GDN_WIDTH), tok),
                   pl.BlockSpec((tm, LANES), tok),
                   pl.BlockSpec((tm, MEM_WIDTH), tok)],
        out_shape=[jax.ShapeDtypeStruct((t, w3), F32),
                   jax.ShapeDtypeStruct((t, GDN_WIDTH), F32),
                   jax.ShapeDtypeStruct((t, LANES), F32),
                   jax.ShapeDtypeStruct((t, MEM_WIDTH), BF16)],
        scratch_shapes=[pltpu.VMEM((d, n), BF16)],
        compiler_params=_params("arbitrary"),
        name="gdn_in",
    )(h, gain.reshape(1, d), w_in)


def _gdn_core_kernel(x_ref, xp_ref, ab_ref, z_ref, cw_ref, alog_ref, dtb_ref, on_ref, o_ref, s_ref,
                     *, blocks_per_seq):
    c = GDN_CHUNK
    nchunk = x_ref.shape[0] // c
    i = pl.program_id(0)
    first = (i % blocks_per_seq) == 0

    @pl.when(first)
    def _():
        s_ref[...] = jnp.zeros_like(s_ref)

    x = x_ref[...]
    prev = jnp.where(first, 0.0, xp_ref[...])
    xs = jnp.concatenate([prev, x], axis=0)
    ln = x.shape[0]
    y = cw_ref[CONV_WIDTH - 1:CONV_WIDTH, :] * x
    for tap in range(CONV_WIDTH - 1):
        shift = CONV_WIDTH - 1 - tap
        y = y + cw_ref[tap:tap + 1, :] * xs[SUBLANES - shift:SUBLANES - shift + ln, :]
    y = y * jax.nn.sigmoid(y)

    ab = ab_ref[...]
    pre = ab + dtb_ref[...]
    softplus = jnp.maximum(pre, 0.0) + jnp.log(1.0 + jnp.exp(-jnp.abs(pre)))
    g_col = -jnp.exp(alog_ref[...]) * softplus
    beta_col = jax.nn.sigmoid(ab)

    row = lax.broadcasted_iota(jnp.int32, (c, c), 0)
    col = lax.broadcasted_iota(jnp.int32, (c, c), 1)
    lower = row >= col
    tri = lower.astype(BF16)

    mats, rhss, egs, qhs, qks, kts = [], [], [], [], [], []
    for ci in range(nchunk):
        r0 = ci * c
        gc_col = _tri_cumsum(tri, g_col[r0:r0 + c, :])
        gc_row = gc_col.T
        egs.append(jnp.exp(jnp.broadcast_to(gc_row[0:2 * SUBLANES, c - 1:c], (2 * SUBLANES, LANES))))
        for h in range(GDN_HEADS):
            q = y[r0:r0 + c, h * HEAD_DIM:(h + 1) * HEAD_DIM]
            k = y[r0:r0 + c, GDN_WIDTH + h * HEAD_DIM:GDN_WIDTH + (h + 1) * HEAD_DIM]
            v = y[r0:r0 + c, 2 * GDN_WIDTH + h * HEAD_DIM:2 * GDN_WIDTH + (h + 1) * HEAD_DIM]
            q = q * lax.rsqrt(jnp.sum(q * q, axis=-1, keepdims=True) + EPS) * (HEAD_DIM ** -0.5)
            k = k * lax.rsqrt(jnp.sum(k * k, axis=-1, keepdims=True) + EPS)
            beta = beta_col[r0:r0 + c, GDN_HEADS + h:GDN_HEADS + h + 1]
            gcl = gc_col[:, h:h + 1]
            gr = gc_row[h:h + 1, :]
            glast = gr[:, c - 1:c]
            decay = jnp.where(lower, jnp.exp(jnp.where(lower, gcl - gr, 0.0)), 0.0)
            kb = k * beta
            kbf = k.astype(BF16)
            mats.append(jnp.where(row > col, _dot_nt(kb.astype(BF16), kbf) * decay, 0.0))
            qk = _dot_nt(q.astype(BF16), kbf) * decay
            egc = jnp.exp(gcl)
            rhss.append(jnp.concatenate([v * beta, kb * egc], axis=1).astype(BF16))
            qhs.append((q * egc).astype(BF16))
            qks.append(qk.astype(BF16))
            kts.append((k * jnp.exp(glast - gcl)).T.astype(BF16))

    sols = [_dot(t.astype(BF16), rhs) for t, rhs in zip(_unit_lower_inverses(mats), rhss)]

    gain = on_ref[...]
    for ci in range(nchunk):
        rows = slice(ci * c, (ci + 1) * c)
        eg = egs[ci]
        probs = range(ci * GDN_HEADS, (ci + 1) * GDN_HEADS)
        states = [s_ref[h] for h in range(GDN_HEADS)]
        sbs = [s.astype(BF16) for s in states]
        ws = [_dot(sols[p][:, HEAD_DIM:].astype(BF16), sb) for p, sb in zip(probs, sbs)]
        vbs = [(sols[p][:, :HEAD_DIM] - ws_h).astype(BF16) for p, ws_h in zip(probs, ws)]
        for h, (p, state, vb) in enumerate(zip(probs, states, vbs)):
            s_ref[h] = state * eg[h:h + 1, :] + _dot(kts[p], vb)
        outs = [_dot(qhs[p], sb) + _dot(qks[p], vb) for p, sb, vb in zip(probs, sbs, vbs)]
        for h, out in enumerate(outs):
            cs = slice(h * HEAD_DIM, (h + 1) * HEAD_DIM)
            zz = z_ref[rows, cs]
            o_ref[rows, cs] = (_rms(out, gain) * (zz * jax.nn.sigmoid(zz))).astype(o_ref.dtype)


def gdn_core(qkv, ab, z, conv_w, alog_row, dtb_row, out_norm, seq, tl):
    t = qkv.shape[0]
    w3 = qkv.shape[1]
    width = w3 // 3
    nblk = t // tl
    prev_blocks = tl // SUBLANES
    tok = lambda i: (i, 0)
    const = lambda i: (0, 0)
    return pl.pallas_call(
        functools.partial(_gdn_core_kernel, blocks_per_seq=seq // tl),
        grid=(nblk,),
        in_specs=[pl.BlockSpec((tl, w3), tok),
                  pl.BlockSpec((SUBLANES, w3), lambda i: (jnp.maximum(i * prev_blocks - 1, 0), 0)),
                  pl.BlockSpec((tl, LANES), tok),
                  pl.BlockSpec((tl, width), tok),
                  pl.BlockSpec((CONV_WIDTH, w3), const),
                  pl.BlockSpec((1, LANES), const),
                  pl.BlockSpec((1, LANES), const),
                  pl.BlockSpec((1, HEAD_DIM), const)],
        out_specs=pl.BlockSpec((tl, width), tok),
        out_shape=jax.ShapeDtypeStruct((t, width), BF16),
        scratch_shapes=[pltpu.VMEM((GDN_HEADS, HEAD_DIM, HEAD_DIM), F32)],
        compiler_params=_params("arbitrary"),
        name="gdn_core",
    )(qkv, qkv, ab, z, conv_w, alog_row, dtb_row, out_norm.reshape(1, HEAD_DIM))


def _forget_cumsum_kernel(f_ref, b_ref, k_ref, cfm_ref, kp_ref, carry_ref):
    @pl.when(pl.program_id(1) == 0)
    def _():
        carry_ref[...] = jnp.zeros_like(carry_ref)

    n = f_ref.shape[0]
    pre = f_ref[...] + b_ref[...]
    logf = jnp.minimum(pre, 0.0) - jnp.log(1.0 + jnp.exp(-jnp.abs(pre)))
    row = lax.broadcasted_iota(jnp.int32, (n, n), 0)
    col = lax.broadcasted_iota(jnp.int32, (n, n), 1)
    cs = _tri_cumsum((row >= col).astype(BF16), logf) + carry_ref[...]
    carry_ref[...] = cs[n - 1:n, :]
    cfm_ref[0] = cs.T[:SUBLANES, :]
    hi, mid, lo = (p.astype(F32) for p in _split3(-LOG2E * cs))
    lane = lax.broadcasted_iota(jnp.int32, (n, LANES), 1)
    for h in range(FOX_HEADS):
        aug = jnp.where(lane < 3, 1.0,
                        jnp.where(lane == 3, hi[:, h:h + 1],
                                  jnp.where(lane == 4, mid[:, h:h + 1],
                                            jnp.where(lane == 5, lo[:, h:h + 1], 0.0))))
        kp_ref[h, :, :HEAD_DIM] = k_ref[:, h * HEAD_DIM:(h + 1) * HEAD_DIM]
        kp_ref[h, :, HEAD_DIM:] = aug[:, :FOX_AUG].astype(BF16)


def forget_cumsum(flog, bias_row, k, batch, tl):
    t, width = k.shape
    seq = t // batch
    nblk = seq // tl
    tok = lambda b, i: (b * nblk + i, 0)
    return pl.pallas_call(
        _forget_cumsum_kernel,
        grid=(batch, nblk),
        in_specs=[pl.BlockSpec((tl, LANES), tok),
                  pl.BlockSpec((1, LANES), lambda b, i: (0, 0)),
                  pl.BlockSpec((tl, width), tok)],
        out_specs=[pl.BlockSpec((1, SUBLANES, tl), lambda b, i: (b, 0, i)),
                   pl.BlockSpec((FOX_HEADS, tl, HEAD_DIM + FOX_AUG), lambda b, i: (0, b * nblk + i, 0))],
        out_shape=[jax.ShapeDtypeStruct((batch, SUBLANES, seq), F32),
                   jax.ShapeDtypeStruct((FOX_HEADS, t, HEAD_DIM + FOX_AUG), BF16)],
        scratch_shapes=[pltpu.VMEM((1, LANES), F32)],
        compiler_params=_params("parallel", "arbitrary"),
        name="forget_cumsum",
    )(flog, bias_row, k)


def _fox_in_kernel(x_ref, g_ref, w_ref, cfm_ref, qp_ref, qm_ref):
    xn = _rms(x_ref[...], g_ref[...]).astype(BF16)
    qm_ref[...] = _dot(xn, w_ref[:, FOX_WIDTH:]).astype(qm_ref.dtype)
    qt = (_dot(xn, w_ref[:, :FOX_WIDTH]) * (LOG2E * HEAD_DIM ** -0.5)).T
    tm = qt.shape[1]
    hi, mid, lo = (p.astype(F32) for p in _split3(LOG2E * cfm_ref[0]))
    r = lax.broadcasted_iota(jnp.int32, (FOX_AUG, tm), 0)
    for h in range(FOX_HEADS):
        aug = jnp.where(r == 0, hi[h:h + 1, :],
                        jnp.where(r == 1, mid[h:h + 1, :],
                                  jnp.where(r == 2, lo[h:h + 1, :],
                                            jnp.where(r < 6, 1.0, 0.0))))
        qp_ref[h, :HEAD_DIM, :] = qt[h * HEAD_DIM:(h + 1) * HEAD_DIM, :].astype(BF16)
        qp_ref[h, HEAD_DIM:, :] = aug.astype(BF16)


def fox_in(x, gain, w, cfm, batch, tm):
    t, d = x.shape
    per_batch = t // batch // tm
    return pl.pallas_call(
        _fox_in_kernel,
        grid=(t // tm,),
        in_specs=[pl.BlockSpec((tm, d), lambda i: (i, 0)),
                  pl.BlockSpec((1, d), lambda i: (0, 0)),
                  pl.BlockSpec(w.shape, lambda i: (0, 0)),
                  pl.BlockSpec((1, SUBLANES, tm), lambda i: (i // per_batch, 0, i % per_batch))],
        out_specs=[pl.BlockSpec((FOX_HEADS, HEAD_DIM + FOX_AUG, tm), lambda i: (0, 0, i)),
                   pl.BlockSpec((tm, MEM_WIDTH), lambda i: (i, 0))],
        out_shape=[jax.ShapeDtypeStruct((FOX_HEADS, HEAD_DIM + FOX_AUG, t), BF16),
                   jax.ShapeDtypeStruct((t, MEM_WIDTH), BF16)],
        compiler_params=_params("parallel"),
        name="fox_in",
    )(x, gain.reshape(1, d), w, cfm)


def _fox_kernel(it_ref, jt_ref, kind_ref, first_ref, qp_ref, kp_ref, vt_ref, o_ref, m_ref, acc_ref,
                *, tq, tk, nsteps):
    del it_ref, jt_ref
    step = pl.program_id(0) * nsteps + pl.program_id(1)
    kind = kind_ref[step]

    @pl.when(first_ref[step] == 1)
    def _():
        m_ref[...] = jnp.full_like(m_ref, NEG_INF)
        acc_ref[...] = jnp.zeros_like(acc_ref)

    ones_rows = (lax.broadcasted_iota(jnp.int32, (FOX_AUG, tk), 0) == 0).astype(BF16)

    def heads(masked):
        if masked:
            causal = (lax.broadcasted_iota(jnp.int32, (tk, tq), 0)
                      <= lax.broadcasted_iota(jnp.int32, (tk, tq), 1))
        s_next = _dot(kp_ref[0], qp_ref[0])
        for h in range(FOX_HEADS):
            s = s_next
            if h + 1 < FOX_HEADS:
                s_next = _dot(kp_ref[h + 1], qp_ref[h + 1])
            if masked:
                s = jnp.where(causal, s, NEG_INF)
            m_prev = m_ref[h:h + 1, :]
            m_next = jnp.maximum(m_prev, jnp.max(s, axis=0, keepdims=True))
            alpha = jnp.exp2(m_prev - m_next)
            p = jnp.exp2(s - m_next).astype(BF16)
            vt = jnp.concatenate([vt_ref[h * HEAD_DIM:(h + 1) * HEAD_DIM, :], ones_rows], axis=0)
            acc_ref[h] = alpha * acc_ref[h] + _dot(vt, p)
            m_ref[h:h + 1, :] = m_next

    @pl.when(kind == 1)
    def _():
        heads(False)

    @pl.when(kind == 2)
    def _():
        heads(True)
        for h in range(FOX_HEADS):
            acc = acc_ref[h]
            out = acc[:HEAD_DIM, :] / acc[HEAD_DIM:HEAD_DIM + 1, :]
            o_ref[:, h * HEAD_DIM:(h + 1) * HEAD_DIM] = out.T.astype(o_ref.dtype)


def _fox_bounds_kernel(qp_ref, kp_ref, cfm_ref, bnd_ref, cend_ref, kmax_ref):
    @pl.when(pl.program_id(1) == 0)
    def _():
        kmax_ref[...] = jnp.zeros_like(kmax_ref)

    tq = qp_ref.shape[2]
    c2 = LOG2E * cfm_ref[0]
    bnds, cends = [], []
    for h in range(FOX_HEADS):
        q = qp_ref[h, :HEAD_DIM, :].astype(F32)
        k = kp_ref[h, :, :HEAD_DIM].astype(F32)
        k_norm = jnp.sqrt(jnp.max(jnp.sum(k * k, axis=1, keepdims=True), axis=0, keepdims=True))
        kmax = jnp.maximum(kmax_ref[h:h + 1, :], k_norm)
        kmax_ref[h:h + 1, :] = kmax
        q_norm = jnp.sqrt(jnp.sum(q * q, axis=0, keepdims=True))
        diag = jnp.sum(q * k.T, axis=0, keepdims=True)
        c2h = c2[h:h + 1, :]
        worst = jnp.max(q_norm * kmax[:, :1] + c2h - diag, axis=1, keepdims=True)
        bnds.append(jnp.broadcast_to(worst + FOX_SKIP_LOG2, (1, LANES)))
        cends.append(jnp.broadcast_to(c2h[:, tq - 1:tq], (1, LANES)))
    pad = [jnp.zeros((SUBLANES - FOX_HEADS, LANES), F32)]
    bnd_ref[0, 0] = jnp.concatenate(bnds + pad, axis=0)
    cend_ref[0, 0] = jnp.concatenate(cends + pad, axis=0)


def fox_bounds(qp, kp, cfm, batch, tq):
    heads, kdim, t = qp.shape
    nq = t // batch // tq
    out = jax.ShapeDtypeStruct((batch, nq, SUBLANES, LANES), F32)
    return pl.pallas_call(
        _fox_bounds_kernel,
        grid=(batch, nq),
        in_specs=[pl.BlockSpec((heads, kdim, tq), lambda b, i: (0, 0, b * nq + i)),
                  pl.BlockSpec((heads, tq, kdim), lambda b, i: (0, b * nq + i, 0)),
                  pl.BlockSpec((1, SUBLANES, tq), lambda b, i: (b, 0, i))],
        out_specs=[pl.BlockSpec((1, 1, SUBLANES, LANES), lambda b, i: (b, i, 0, 0)),
                   pl.BlockSpec((1, 1, SUBLANES, LANES), lambda b, i: (b, i, 0, 0))],
        out_shape=[out, out],
        scratch_shapes=[pltpu.VMEM((SUBLANES, LANES), F32)],
        compiler_params=_params("parallel", "arbitrary"),
        name="fox_bounds",
    )(qp, kp, cfm)


def _fox_step_tables(bnd, cend, nq):
    nsteps = nq * (nq + 1) // 2
    blk = jnp.arange(nq, dtype=jnp.int32)
    bnd = bnd[:, :, :FOX_HEADS, 0]
    cend = cend[:, :, :FOX_HEADS, 0]
    negligible = jnp.all(cend[:, None, :, :] >= bnd[:, :, None, :], axis=-1)
    needed = (~negligible | (blk[None, :] >= blk[:, None])[None]).astype(jnp.int32)
    j_first = jnp.sum((jnp.cumsum(needed, axis=2) == 0).astype(jnp.int32), axis=2)
    count = blk[None, :] + 1 - j_first
    end = jnp.cumsum(count, axis=1)
    start = end - count
    slot = jnp.arange(nsteps, dtype=jnp.int32)[None, :, None]
    it = jnp.minimum(jnp.sum((slot >= end[:, None, :]).astype(jnp.int32), axis=2), nq - 1)
    mine = (it[:, :, None] == blk[None, None, :]).astype(jnp.int32)
    jt = jnp.minimum(jnp.sum(mine * (j_first - start)[:, None, :], axis=2) + slot[:, :, 0], it)
    valid = (slot[:, :, 0] < end[:, nq - 1:nq]).astype(jnp.int32)
    kind = valid * (1 + (jt == it).astype(jnp.int32))
    first = valid * jnp.sum(mine * (start[:, None, :] == slot).astype(jnp.int32), axis=2)
    return it.reshape(-1), jt.reshape(-1), kind.reshape(-1), first.reshape(-1), nsteps


def fox_attention(qp, kp, vt, cfm, batch, tq):
    heads, kdim, t = qp.shape
    width = vt.shape[0]
    seq = t // batch
    nq = seq // tq
    tk = tq
    bnd, cend = fox_bounds(qp, kp, cfm, batch, tq)
    i_tab, j_tab, kind, first, nsteps = _fox_step_tables(bnd, cend, nq)
    qblk = lambda b, s, it, jt, kd, fs: b * nq + it[b * nsteps + s]
    kblk = lambda b, s, it, jt, kd, fs: b * nq + jt[b * nsteps + s]
    grid_spec = pltpu.PrefetchScalarGridSpec(
        num_scalar_prefetch=4,
        grid=(batch, nsteps),
        in_specs=[pl.BlockSpec((heads, kdim, tq), lambda *a: (0, 0, qblk(*a))),
                  pl.BlockSpec((heads, tk, kdim), lambda *a: (0, kblk(*a), 0)),
                  pl.BlockSpec((width, tk), lambda *a: (0, kblk(*a)))],
        out_specs=pl.BlockSpec((tq, width), lambda *a: (qblk(*a), 0)),
        scratch_shapes=[pltpu.VMEM((SUBLANES, tq), F32),
                        pltpu.VMEM((heads, kdim, tq), F32)])
    return pl.pallas_call(
        functools.partial(_fox_kernel, tq=tq, tk=tk, nsteps=nsteps),
        grid_spec=grid_spec,
        out_shape=jax.ShapeDtypeStruct((t, width), BF16),
        compiler_params=_params("parallel", "arbitrary"),
        name="fox_attention",
    )(i_tab, j_tab, kind, first, qp, kp, vt)


def _mix_out_kernel(h_ref, main_ref, qm_ref, kbd_ref, vbd_ref, wo_ref, o_ref, *, main_width, mem_len):
    s = _dot(qm_ref[...], kbd_ref[0]) * (MEM_HEAD_DIM ** -0.5)
    probs = []
    for g in range(MEM_HEADS):
        sg = s[:, g * mem_len:(g + 1) * mem_len]
        p = jnp.exp(sg - jnp.max(sg, axis=1, keepdims=True))
        probs.append((p / jnp.sum(p, axis=1, keepdims=True)).astype(BF16))
    mem_out = _dot(jnp.concatenate(probs, axis=1), vbd_ref[0])
    o_ref[...] = (h_ref[...] + _dot(main_ref[...], wo_ref[:main_width, :])
                  + _dot(mem_out.astype(BF16), wo_ref[main_width:, :]))


def mix_out(h, main, qmem, kbd, vbd, wo, layer, batch, tm):
    t, d = h.shape
    main_width = main.shape[1]
    mem_len = kbd.shape[2] // MEM_HEADS
    per_batch = t // batch // tm
    tok = lambda i: (i, 0)
    return pl.pallas_call(
        functools.partial(_mix_out_kernel, main_width=main_width, mem_len=mem_len),
        grid=(t // tm,),
        in_specs=[pl.BlockSpec((tm, d), tok),
                  pl.BlockSpec((tm, main_width), tok),
                  pl.BlockSpec((tm, MEM_WIDTH), tok),
                  pl.BlockSpec((1,) + kbd.shape[1:], lambda i: (i // per_batch, 0, 0)),
                  pl.BlockSpec((1,) + vbd.shape[1:], lambda i: (i // per_batch, 0, 0)),
                  pl.BlockSpec((None,) + wo.shape[1:], lambda i: (layer, 0, 0))],
        out_specs=pl.BlockSpec((tm, d), tok),
        out_shape=jax.ShapeDtypeStruct((t, d), F32),
        compiler_params=_params("parallel"),
        name="mix_out",
    )(h, main, qmem, kbd, vbd, wo)


def _pad_cols(w, n):
    return jnp.pad(w, ((0, 0), (0, n - w.shape[1])))


def _lane_row(vec, offset=0):
    return jnp.zeros((1, LANES), F32).at[0, offset:offset + vec.shape[0]].set(vec.astype(F32))


def _memory_block_diag(kv, batch, mem_len):
    kv = kv.reshape(batch, mem_len, 2 * MEM_WIDTH)
    k, v = kv[..., :MEM_WIDTH], kv[..., MEM_WIDTH:]
    head_of_feat = jnp.arange(MEM_WIDTH) // MEM_HEAD_DIM
    head_of_slot = jnp.arange(MEM_HEADS * mem_len) // mem_len
    mask = head_of_feat[:, None] == head_of_slot[None, :]
    kbd = jnp.where(mask[None], jnp.tile(k.transpose(0, 2, 1), (1, 1, MEM_HEADS)), 0).astype(BF16)
    vbd = jnp.where(mask.T[None], jnp.tile(v, (1, MEM_HEADS, 1)), 0).astype(BF16)
    return kbd, vbd


def kernel(x, mem, ffn1_norm, ffn1_w_gate_up, ffn1_w_down, mix_norm, ffn2_norm, ffn2_w_gate_up,
           ffn2_w_down, gdn_w_in, gdn_conv, gdn_A_log, gdn_dt_bias, gdn_out_norm, fox_w_in, w_out,
           mem_norm, mem_w_kv, kv_norm, kv_w, kv_b_f, final_norm):
    bsz, seq, d = x.shape
    mem_len = mem.shape[1]
    depth = ffn1_norm.shape[0]
    n_a = gdn_w_in.shape[0]
    t = bsz * seq
    tm = min(512, seq)
    tf = 256
    tl = min(256, seq)
    tq = min(512, seq)

    h = x.reshape(t, d).astype(F32)

    w_kv_all = jnp.concatenate([mem_w_kv[l] for l in range(depth)], axis=1).astype(BF16)
    (kv_mem,) = norm_matmul(mem.reshape(bsz * mem_len, d).astype(F32), mem_norm, w_kv_all,
                            [(0, w_kv_all.shape[1], False)], [F32], min(256, bsz * mem_len), "mem_kv")

    o0, o1 = 3 * GDN_WIDTH, 4 * GDN_WIDTH
    kp = vt = cfm = None
    wgu1, wd1 = ffn1_w_gate_up.astype(BF16), ffn1_w_down.astype(BF16)
    wgu2, wd2 = ffn2_w_gate_up.astype(BF16), ffn2_w_down.astype(BF16)
    wo_all = w_out.astype(BF16)
    for l in range(depth):
        last = l == depth - 1
        h = ffn(h, ffn1_norm[l], wgu1, wd1, l, final_norm, tm, tf, False, "ffn1")
        kbd, vbd = _memory_block_diag(kv_mem[:, 2 * MEM_WIDTH * l:2 * MEM_WIDTH * (l + 1)], bsz, mem_len)
        if l < n_a:
            qkv, z, ab, qmem = gdn_in(h, mix_norm[l], gdn_w_in.astype(F32), l, tm)
            alog_row = _lane_row(gdn_A_log[l])
            dtb_row = _lane_row(gdn_dt_bias[l])
            main = gdn_core(qkv, ab, z, gdn_conv[l].astype(F32), alog_row, dtb_row,
                            gdn_out_norm[l].astype(F32), seq, tl)
        else:
            qp, qmem = fox_in(h, mix_norm[l], fox_w_in[l - n_a].astype(BF16), cfm, bsz, tm)
            main = fox_attention(qp, kp, vt, cfm, bsz, tq)
        h = mix_out(h, main, qmem, kbd, vbd, wo_all, l, bsz, tm)
        h = ffn(h, ffn2_norm[l], wgu2, wd2, l, final_norm, tm, tf, last, "ffn2")
        if l == n_a - 1:
            w_cat = jnp.concatenate([kv_w[:, :2 * FOX_WIDTH], _pad_cols(kv_w[:, 2 * FOX_WIDTH:], LANES)],
                                    axis=1).astype(BF16)
            k_sh, vt, flog = norm_matmul(
                h, kv_norm, w_cat,
                [(0, FOX_WIDTH, False), (FOX_WIDTH, 2 * FOX_WIDTH, True),
                 (2 * FOX_WIDTH, 2 * FOX_WIDTH + LANES, False)],
                [BF16, BF16, F32], tm, "shared_kv")
            cfm, kp = forget_cumsum(flog, _lane_row(kv_b_f), k_sh, bsz, tm)
    return h.reshape(bsz, seq, d)
```

```python
import functools

import jax
import jax.numpy as jnp
from jax import lax
from jax.experimental import pallas as pl
from jax.experimental.pallas import tpu as pltpu

F32 = jnp.float32
BF16 = jnp.bfloat16

EPS = 1e-6
NEG_INF = -1e30
HEAD_DIM = 128
GDN_HEADS = 6
GDN_WIDTH = GDN_HEADS * HEAD_DIM
FOX_HEADS = 6
FOX_WIDTH = FOX_HEADS * HEAD_DIM
CONV_WIDTH = 4
MEM_HEADS = 4
MEM_HEAD_DIM = 64
MEM_WIDTH = MEM_HEADS * MEM_HEAD_DIM

LANES = 128
SUBLANES = 8
GDN_CHUNK = 128
FOX_AUG = 16
LOG2E = 1.4426950408889634
FOX_SKIP_LOG2 = 160.0
VMEM_LIMIT = 56 * 1024 * 1024


def _params(*sem):
    return pltpu.CompilerParams(dimension_semantics=sem, vmem_limit_bytes=VMEM_LIMIT)


def _dot(a, b):
    return jnp.dot(a, b, preferred_element_type=F32)


def _dot_nt(a, b):
    return lax.dot_general(a, b, (((1,), (1,)), ((), ())), preferred_element_type=F32)


def _rms(x, gain):
    return x * lax.rsqrt(jnp.mean(x * x, axis=-1, keepdims=True) + EPS) * gain


def _split3(x):
    hi = x.astype(BF16)
    r1 = x - hi.astype(F32)
    mid = r1.astype(BF16)
    lo = (r1 - mid.astype(F32)).astype(BF16)
    return hi, mid, lo


def _tri_cumsum(tri, x):
    hi, mid, lo = _split3(x)
    return _dot(tri, hi) + _dot(tri, mid) + _dot(tri, lo)


def _norm_matmul_kernel(x_ref, g_ref, w_ref, *out_refs, splits):
    xn = _rms(x_ref[...], g_ref[...]).astype(BF16)
    for (lo, hi, transposed), o_ref in zip(splits, out_refs):
        y = _dot(xn, w_ref[:, lo:hi])
        if transposed:
            o_ref[...] = y.T.astype(o_ref.dtype)
        else:
            o_ref[...] = y.astype(o_ref.dtype)


def norm_matmul(x, gain, w, splits, dtypes, tm, name):
    t, d = x.shape
    n = w.shape[1]
    out_shape, out_specs = [], []
    for (lo, hi, transposed), dt in zip(splits, dtypes):
        if transposed:
            out_shape.append(jax.ShapeDtypeStruct((hi - lo, t), dt))
            out_specs.append(pl.BlockSpec((hi - lo, tm), lambda i: (0, i)))
        else:
            out_shape.append(jax.ShapeDtypeStruct((t, hi - lo), dt))
            out_specs.append(pl.BlockSpec((tm, hi - lo), lambda i: (i, 0)))
    return pl.pallas_call(
        functools.partial(_norm_matmul_kernel, splits=tuple(splits)),
        grid=(t // tm,),
        in_specs=[pl.BlockSpec((tm, d), lambda i: (i, 0)),
                  pl.BlockSpec((1, d), lambda i: (0, 0)),
                  pl.BlockSpec((d, n), lambda i: (0, 0))],
        out_specs=out_specs,
        out_shape=out_shape,
        compiler_params=_params("parallel"),
        name=name,
    )(x, gain.reshape(1, d), w)


def _ffn_kernel(h_ref, g_ref, wgu_ref, wd_ref, fg_ref, o_ref, act_ref, *, hidden, tf, final_norm):
    x = h_ref[...]
    xn = _rms(x, g_ref[...]).astype(BF16)
    for j in range(hidden // tf):
        gate = _dot(xn, wgu_ref[:, j * tf:(j + 1) * tf])
        up = _dot(xn, wgu_ref[:, hidden + j * tf:hidden + (j + 1) * tf])
        act_ref[:, j * tf:(j + 1) * tf] = (gate * jax.nn.sigmoid(gate) * up).astype(BF16)
    out = x + 0.5 * _dot(act_ref[...], wd_ref[...])
    if final_norm:
        out = _rms(out, fg_ref[...])
    o_ref[...] = out


def ffn(h, gain, wgu, wd, layer, final_gain, tm, tf, final_norm, name):
    t, d = h.shape
    hidden = wd.shape[1]
    return pl.pallas_call(
        functools.partial(_ffn_kernel, hidden=hidden, tf=tf, final_norm=final_norm),
        grid=(t // tm,),
        in_specs=[pl.BlockSpec((tm, d), lambda i: (i, 0)),
                  pl.BlockSpec((1, d), lambda i: (0, 0)),
                  pl.BlockSpec((None, d, 2 * hidden), lambda i: (layer, 0, 0), pipeline_mode=pl.Buffered(1)),
                  pl.BlockSpec((None, hidden, d), lambda i: (layer, 0, 0), pipeline_mode=pl.Buffered(1)),
                  pl.BlockSpec((1, d), lambda i: (0, 0))],
        out_specs=pl.BlockSpec((tm, d), lambda i: (i, 0)),
        out_shape=jax.ShapeDtypeStruct((t, d), F32),
        scratch_shapes=[pltpu.VMEM((tm, hidden), BF16)],
        compiler_params=_params("parallel"),
        name=name,
    )(h, gain.reshape(1, d), wgu, wd, final_gain.reshape(1, d))


def _unit_lower_inverses(mats):
    n = mats[0].shape[0]
    row = lax.broadcasted_iota(jnp.int32, (n, n), 0)
    col = lax.broadcasted_iota(jnp.int32, (n, n), 1)
    eye = (row == col).astype(F32)
    pair = (row >> 1) == (col >> 1)
    ts = [eye - jnp.where(pair, a, 0.0) for a in mats]
    s = 2
    while s < n:
        k = s.bit_length() - 1
        sub = ((row >> (k + 1)) == (col >> (k + 1))) & ((row >> k) != (col >> k))
        tbs = [t.astype(BF16) for t in ts]
        tes = [_dot(tb, jnp.where(sub, a, 0.0).astype(BF16)).astype(BF16) for tb, a in zip(tbs, mats)]
        ts = [t - _dot(te, tb) for t, te, tb in zip(ts, tes, tbs)]
        s *= 2
    return ts


def _conv_silu(x, prev, cw):
    ln = x.shape[0]
    xs = jnp.concatenate([prev, x], axis=0)
    y = cw[CONV_WIDTH - 1:CONV_WIDTH, :] * x
    for tap in range(CONV_WIDTH - 1):
        shift = CONV_WIDTH - 1 - tap
        y = y + cw[tap:tap + 1, :] * xs[SUBLANES - shift:SUBLANES - shift + ln, :]
    return y * jax.nn.sigmoid(y)


def _gdn_in_kernel(x_ref, g_ref, w_ref, qkv_ref, z_ref, ab_ref, qm_ref, wb_ref):
    @pl.when(pl.program_id(0) == 0)
    def _():
        wb_ref[...] = w_ref[...].astype(BF16)

    o0, o1 = 3 * GDN_WIDTH, 4 * GDN_WIDTH
    ab_off = 2 * GDN_HEADS
    xn = _rms(x_ref[...], g_ref[...]).astype(BF16)
    qkv_ref[...] = _dot(xn, wb_ref[:, :o0])
    z_ref[...] = _dot(xn, wb_ref[:, o0:o1])
    rest = _dot(xn, wb_ref[:, o1:])
    ab_ref[...] = rest[:, :LANES]
    qm_ref[...] = rest[:, ab_off:ab_off + MEM_WIDTH].astype(qm_ref.dtype)


def gdn_in(h, gain, w_in, layer, tm):
    t, d = h.shape
    n = w_in.shape[2]
    w3 = 3 * GDN_WIDTH
    tok = lambda i: (i, 0)
    const = lambda i: (0, 0)
    return pl.pallas_call(
        _gdn_in_kernel,
        grid=(t // tm,),
        in_specs=[pl.BlockSpec((tm, d), tok),
                  pl.BlockSpec((1, d), const),
                  pl.BlockSpec((None, d, n), lambda i: (layer, 0, 0), pipeline_mode=pl.Buffered(1))],
        out_specs=[pl.BlockSpec((tm, w3), tok),
                   pl.BlockSpec((tm, GDN_WIDTH), tok),
                   pl.BlockSpec((tm, LANES), tok),
                   pl.BlockSpec((tm, MEM_WIDTH), tok)],
        out_shape=[jax.ShapeDtypeStruct((t, w3), F32),
                   jax.ShapeDtypeStruct((t, GDN_WIDTH), F32),
                   jax.ShapeDtypeStruct((t, LANES), F32),
                   jax.ShapeDtypeStruct((t, MEM_WIDTH), BF16)],
        scratch_shapes=[pltpu.VMEM((d, n), BF16)],
        compiler_params=_params("arbitrary"),
        name="gdn_in",
    )(h, gain.reshape(1, d), w_in)


def _gdn_core_kernel(x_ref, xp_ref, ab_ref, z_ref, cw_ref, alog_ref, dtb_ref, on_ref, o_ref, s_ref,
                     *, blocks_per_seq):
    c = GDN_CHUNK
    nchunk = x_ref.shape[0] // c
    first = (pl.program_id(0) % blocks_per_seq) == 0

    @pl.when(first)
    def _():
        s_ref[...] = jnp.zeros_like(s_ref)

    prev = jnp.where(first, 0.0, xp_ref[...])
    y = _conv_silu(x_ref[...], prev, cw_ref[...])

    ab = ab_ref[...]
    pre = ab + dtb_ref[...]
    softplus = jnp.maximum(pre, 0.0) + jnp.log(1.0 + jnp.exp(-jnp.abs(pre)))
    g_col = -jnp.exp(alog_ref[...]) * softplus
    beta_col = jax.nn.sigmoid(ab)

    row = lax.broadcasted_iota(jnp.int32, (c, c), 0)
    col = lax.broadcasted_iota(jnp.int32, (c, c), 1)
    lower = row >= col
    tri = lower.astype(BF16)

    mats, rhss, egs, qhs, qks, kts = [], [], [], [], [], []
    for ci in range(nchunk):
        r0 = ci * c
        gc_col = _tri_cumsum(tri, g_col[r0:r0 + c, :])
        gc_row = gc_col.T
        egs.append(jnp.exp(jnp.broadcast_to(gc_row[0:2 * SUBLANES, c - 1:c], (2 * SUBLANES, LANES))))
        for h in range(GDN_HEADS):
            q = y[r0:r0 + c, h * HEAD_DIM:(h + 1) * HEAD_DIM]
            k = y[r0:r0 + c, GDN_WIDTH + h * HEAD_DIM:GDN_WIDTH + (h + 1) * HEAD_DIM]
            v = y[r0:r0 + c, 2 * GDN_WIDTH + h * HEAD_DIM:2 * GDN_WIDTH + (h + 1) * HEAD_DIM]
            q = q * lax.rsqrt(jnp.sum(q * q, axis=-1, keepdims=True) + EPS) * (HEAD_DIM ** -0.5)
            k = k * lax.rsqrt(jnp.sum(k * k, axis=-1, keepdims=True) + EPS)
            beta = beta_col[r0:r0 + c, GDN_HEADS + h:GDN_HEADS + h + 1]
            gcl = gc_col[:, h:h + 1]
            gr = gc_row[h:h + 1, :]
            glast = gr[:, c - 1:c]
            decay = jnp.where(lower, jnp.exp(jnp.where(lower, gcl - gr, 0.0)), 0.0)
            kb = k * beta
            kbf = k.astype(BF16)
            mats.append(jnp.where(row > col, _dot_nt(kb.astype(BF16), kbf) * decay, 0.0))
            qk = _dot_nt(q.astype(BF16), kbf) * decay
            egc = jnp.exp(gcl)
            rhss.append(jnp.concatenate([v * beta, kb * egc], axis=1).astype(BF16))
            qhs.append((q * egc).astype(BF16))
            qks.append(qk.astype(BF16))
            kts.append((k * jnp.exp(glast - gcl)).T.astype(BF16))

    sols = [_dot(t.astype(BF16), rhs) for t, rhs in zip(_unit_lower_inverses(mats), rhss)]

    gain = on_ref[...]
    for ci in range(nchunk):
        rows = slice(ci * c, (ci + 1) * c)
        eg = egs[ci]
        probs = range(ci * GDN_HEADS, (ci + 1) * GDN_HEADS)
        states = [s_ref[h] for h in range(GDN_HEADS)]
        sbs = [s.astype(BF16) for s in states]
        ws = [_dot(sols[p][:, HEAD_DIM:].astype(BF16), sb) for p, sb in zip(probs, sbs)]
        vbs = [(sols[p][:, :HEAD_DIM] - ws_h).astype(BF16) for p, ws_h in zip(probs, ws)]
        for h, (p, state, vb) in enumerate(zip(probs, states, vbs)):
            s_ref[h] = state * eg[h:h + 1, :] + _dot(kts[p], vb)
        outs = [_dot(qhs[p], sb) + _dot(qks[p], vb) for p, sb, vb in zip(probs, sbs, vbs)]
        for h, out in enumerate(outs):
            cs = slice(h * HEAD_DIM, (h + 1) * HEAD_DIM)
            zz = z_ref[rows, cs]
            o_ref[rows, cs] = (_rms(out, gain) * (zz * jax.nn.sigmoid(zz))).astype(o_ref.dtype)


def gdn_core(qkv, ab, z, conv_w, alog_row, dtb_row, out_norm, seq, tl):
    t = qkv.shape[0]
    w3 = qkv.shape[1]
    width = w3 // 3
    nblk = t // tl
    prev_blocks = tl // SUBLANES
    tok = lambda i: (i, 0)
    const = lambda i: (0, 0)
    return pl.pallas_call(
        functools.partial(_gdn_core_kernel, blocks_per_seq=seq // tl),
        grid=(nblk,),
        in_specs=[pl.BlockSpec((tl, w3), tok),
                  pl.BlockSpec((SUBLANES, w3), lambda i: (jnp.maximum(i * prev_blocks - 1, 0), 0)),
                  pl.BlockSpec((tl, LANES), tok),
                  pl.BlockSpec((tl, width), tok),
                  pl.BlockSpec((CONV_WIDTH, w3), const),
                  pl.BlockSpec((1, LANES), const),
                  pl.BlockSpec((1, LANES), const),
                  pl.BlockSpec((1, HEAD_DIM), const)],
        out_specs=pl.BlockSpec((tl, width), tok),
        out_shape=jax.ShapeDtypeStruct((t, width), BF16),
        scratch_shapes=[pltpu.VMEM((GDN_HEADS, HEAD_DIM, HEAD_DIM), F32)],
        compiler_params=_params("arbitrary"),
        name="gdn_core",
    )(qkv, qkv, ab, z, conv_w, alog_row, dtb_row, out_norm.reshape(1, HEAD_DIM))


def _forget_cumsum_kernel(f_ref, b_ref, k_ref, cfm_ref, kp_ref, kn_ref, carry_ref):
    @pl.when(pl.program_id(1) == 0)
    def _():
        carry_ref[...] = jnp.zeros_like(carry_ref)

    n = f_ref.shape[0]
    pre = f_ref[...] + b_ref[...]
    logf = jnp.minimum(pre, 0.0) - jnp.log(1.0 + jnp.exp(-jnp.abs(pre)))
    row = lax.broadcasted_iota(jnp.int32, (n, n), 0)
    col = lax.broadcasted_iota(jnp.int32, (n, n), 1)
    cs = _tri_cumsum((row >= col).astype(BF16), logf) + carry_ref[...]
    carry_ref[...] = cs[n - 1:n, :]
    cfm_ref[0] = cs.T[:SUBLANES, :]
    hi, mid, lo = (p.astype(F32) for p in _split3(-LOG2E * cs))
    lane = lax.broadcasted_iota(jnp.int32, (n, LANES), 1)
    norms = []
    for h in range(FOX_HEADS):
        aug = jnp.where(lane < 3, 1.0,
                        jnp.where(lane == 3, hi[:, h:h + 1],
                                  jnp.where(lane == 4, mid[:, h:h + 1],
                                            jnp.where(lane == 5, lo[:, h:h + 1], 0.0))))
        kh = k_ref[:, h * HEAD_DIM:(h + 1) * HEAD_DIM]
        kp_ref[h, :, :HEAD_DIM] = kh
        kp_ref[h, :, HEAD_DIM:] = aug[:, :FOX_AUG].astype(BF16)
        kf = kh.astype(F32)
        kn2 = jnp.max(jnp.sum(kf * kf, axis=1, keepdims=True), axis=0, keepdims=True)
        norms.append(jnp.broadcast_to(kn2, (1, LANES)))
    kn_ref[0, 0] = jnp.concatenate(norms + [jnp.zeros((SUBLANES - FOX_HEADS, LANES), F32)], axis=0)


def forget_cumsum(flog, bias_row, k, batch, tl):
    t, width = k.shape
    seq = t // batch
    nblk = seq // tl
    tok = lambda b, i: (b * nblk + i, 0)
    return pl.pallas_call(
        _forget_cumsum_kernel,
        grid=(batch, nblk),
        in_specs=[pl.BlockSpec((tl, LANES), tok),
                  pl.BlockSpec((1, LANES), lambda b, i: (0, 0)),
                  pl.BlockSpec((tl, width), tok)],
        out_specs=[pl.BlockSpec((1, SUBLANES, tl), lambda b, i: (b, 0, i)),
                   pl.BlockSpec((FOX_HEADS, tl, HEAD_DIM + FOX_AUG), lambda b, i: (0, b * nblk + i, 0)),
                   pl.BlockSpec((1, 1, SUBLANES, LANES), lambda b, i: (b, i, 0, 0))],
        out_shape=[jax.ShapeDtypeStruct((batch, SUBLANES, seq), F32),
                   jax.ShapeDtypeStruct((FOX_HEADS, t, HEAD_DIM + FOX_AUG), BF16),
                   jax.ShapeDtypeStruct((batch, nblk, SUBLANES, LANES), F32)],
        scratch_shapes=[pltpu.VMEM((1, LANES), F32)],
        compiler_params=_params("parallel", "arbitrary"),
        name="forget_cumsum",
    )(flog, bias_row, k)


def _fox_in_kernel(x_ref, g_ref, w_ref, cfm_ref, kmax_ref, qp_ref, qm_ref, bnd_ref):
    xn = _rms(x_ref[...], g_ref[...]).astype(BF16)
    qm_ref[...] = _dot(xn, w_ref[:, FOX_WIDTH:]).astype(qm_ref.dtype)
    qt = (_dot(xn, w_ref[:, :FOX_WIDTH]) * (LOG2E * HEAD_DIM ** -0.5)).T
    tm = qt.shape[1]
    c2 = LOG2E * cfm_ref[0]
    hi, mid, lo = (p.astype(F32) for p in _split3(c2))
    kmax = kmax_ref[0, 0]
    r = lax.broadcasted_iota(jnp.int32, (FOX_AUG, tm), 0)
    bnds = []
    for h in range(FOX_HEADS):
        aug = jnp.where(r == 0, hi[h:h + 1, :],
                        jnp.where(r == 1, mid[h:h + 1, :],
                                  jnp.where(r == 2, lo[h:h + 1, :],
                                            jnp.where(r < 6, 1.0, 0.0))))
        qb = qt[h * HEAD_DIM:(h + 1) * HEAD_DIM, :].astype(BF16)
        qp_ref[h, :HEAD_DIM, :] = qb
        qp_ref[h, HEAD_DIM:, :] = aug.astype(BF16)
        qf = qb.astype(F32)
        q_norm = jnp.sqrt(jnp.sum(qf * qf, axis=0, keepdims=True))
        worst = jnp.max(2.0 * q_norm * kmax[h:h + 1, :1] + c2[h:h + 1, :], axis=1, keepdims=True)
        bnds.append(jnp.broadcast_to(worst + FOX_SKIP_LOG2, (1, LANES)))
    bnd_ref[0, 0] = jnp.concatenate(bnds + [jnp.zeros((SUBLANES - FOX_HEADS, LANES), F32)], axis=0)


def fox_in(x, gain, w, cfm, kmax, batch, tm):
    t, d = x.shape
    per_batch = t // batch // tm
    assert kmax.shape[1] == per_batch
    blk = lambda i: (i // per_batch, i % per_batch, 0, 0)
    return pl.pallas_call(
        _fox_in_kernel,
        grid=(t // tm,),
        in_specs=[pl.BlockSpec((tm, d), lambda i: (i, 0)),
                  pl.BlockSpec((1, d), lambda i: (0, 0)),
                  pl.BlockSpec(w.shape, lambda i: (0, 0)),
                  pl.BlockSpec((1, SUBLANES, tm), lambda i: (i // per_batch, 0, i % per_batch)),
                  pl.BlockSpec((1, 1, SUBLANES, LANES), blk)],
        out_specs=[pl.BlockSpec((FOX_HEADS, HEAD_DIM + FOX_AUG, tm), lambda i: (0, 0, i)),
                   pl.BlockSpec((tm, MEM_WIDTH), lambda i: (i, 0)),
                   pl.BlockSpec((1, 1, SUBLANES, LANES), blk)],
        out_shape=[jax.ShapeDtypeStruct((FOX_HEADS, HEAD_DIM + FOX_AUG, t), BF16),
                   jax.ShapeDtypeStruct((t, MEM_WIDTH), BF16),
                   jax.ShapeDtypeStruct((batch, per_batch, SUBLANES, LANES), F32)],
        compiler_params=_params("parallel"),
        name="fox_in",
    )(x, gain.reshape(1, d), w, cfm, kmax)


def _fox_kernel(it_ref, jt_ref, kind_ref, first_ref, qp_ref, kp_ref, vt_ref, o_ref, m_ref, acc_ref,
                *, tq, tk, nsteps):
    del it_ref, jt_ref
    step = pl.program_id(0) * nsteps + pl.program_id(1)
    kind = kind_ref[step]

    @pl.when(first_ref[step] == 1)
    def _():
        m_ref[...] = jnp.full_like(m_ref, NEG_INF)
        acc_ref[...] = jnp.zeros_like(acc_ref)

    ones_rows = (lax.broadcasted_iota(jnp.int32, (FOX_AUG, tk), 0) == 0).astype(BF16)

    def heads(masked):
        if masked:
            causal = (lax.broadcasted_iota(jnp.int32, (tk, tq), 0)
                      <= lax.broadcasted_iota(jnp.int32, (tk, tq), 1))
        s_next = _dot(kp_ref[0], qp_ref[0])
        for h in range(FOX_HEADS):
            s = s_next
            if h + 1 < FOX_HEADS:
                s_next = _dot(kp_ref[h + 1], qp_ref[h + 1])
            if masked:
                s = jnp.where(causal, s, NEG_INF)
            m_prev = m_ref[h:h + 1, :]
            m_next = jnp.maximum(m_prev, jnp.max(s, axis=0, keepdims=True))
            alpha = jnp.exp2(m_prev - m_next)
            p = jnp.exp2(s - m_next).astype(BF16)
            vt = jnp.concatenate([vt_ref[h * HEAD_DIM:(h + 1) * HEAD_DIM, :], ones_rows], axis=0)
            acc_ref[h] = alpha * acc_ref[h] + _dot(vt, p)
            m_ref[h:h + 1, :] = m_next

    @pl.when(kind == 1)
    def _():
        heads(False)

    @pl.when(kind == 2)
    def _():
        heads(True)
        for h in range(FOX_HEADS):
            acc = acc_ref[h]
            out = acc[:HEAD_DIM, :] / acc[HEAD_DIM:HEAD_DIM + 1, :]
            o_ref[:, h * HEAD_DIM:(h + 1) * HEAD_DIM] = out.T.astype(o_ref.dtype)


def _prefix_max_key_norm(kn2):
    nblk = kn2.shape[1]
    blk = jnp.arange(nblk)
    upto = (blk[None, :] <= blk[:, None])[None, :, :, None, None]
    return jnp.sqrt(jnp.max(jnp.where(upto, kn2[:, None], 0.0), axis=2))


def _fox_step_tables(bnd, cfm, nq, tq):
    nsteps = nq * (nq + 1) // 2
    blk = jnp.arange(nq, dtype=jnp.int32)
    bnd = bnd[:, :, :FOX_HEADS, 0]
    cend = (LOG2E * cfm[:, :FOX_HEADS, tq - 1::tq]).transpose(0, 2, 1)
    negligible = jnp.all(cend[:, None, :, :] >= bnd[:, :, None, :], axis=-1)
    needed = (~negligible | (blk[None, :] >= blk[:, None])[None]).astype(jnp.int32)
    j_first = jnp.sum((jnp.cumsum(needed, axis=2) == 0).astype(jnp.int32), axis=2)
    count = blk[None, :] + 1 - j_first
    end = jnp.cumsum(count, axis=1)
    start = end - count
    slot = jnp.arange(nsteps, dtype=jnp.int32)[None, :, None]
    it = jnp.minimum(jnp.sum((slot >= end[:, None, :]).astype(jnp.int32), axis=2), nq - 1)
    mine = (it[:, :, None] == blk[None, None, :]).astype(jnp.int32)
    jt = jnp.minimum(jnp.sum(mine * (j_first - start)[:, None, :], axis=2) + slot[:, :, 0], it)
    valid = (slot[:, :, 0] < end[:, nq - 1:nq]).astype(jnp.int32)
    kind = valid * (1 + (jt == it).astype(jnp.int32))
    first = valid * jnp.sum(mine * (start[:, None, :] == slot).astype(jnp.int32), axis=2)
    return it.reshape(-1), jt.reshape(-1), kind.reshape(-1), first.reshape(-1), nsteps


def fox_attention(qp, kp, vt, bnd, cfm, batch, tq):
    heads, kdim, t = qp.shape
    width = vt.shape[0]
    seq = t // batch
    nq = seq // tq
    tk = tq
    i_tab, j_tab, kind, first, nsteps = _fox_step_tables(bnd, cfm, nq, tq)
    qblk = lambda b, s, it, jt, kd, fs: b * nq + it[b * nsteps + s]
    kblk = lambda b, s, it, jt, kd, fs: b * nq + jt[b * nsteps + s]
    grid_spec = pltpu.PrefetchScalarGridSpec(
        num_scalar_prefetch=4,
        grid=(batch, nsteps),
        in_specs=[pl.BlockSpec((heads, kdim, tq), lambda *a: (0, 0, qblk(*a))),
                  pl.BlockSpec((heads, tk, kdim), lambda *a: (0, kblk(*a), 0)),
                  pl.BlockSpec((width, tk), lambda *a: (0, kblk(*a)))],
        out_specs=pl.BlockSpec((tq, width), lambda *a: (qblk(*a), 0)),
        scratch_shapes=[pltpu.VMEM((SUBLANES, tq), F32),
                        pltpu.VMEM((heads, kdim, tq), F32)])
    return pl.pallas_call(
        functools.partial(_fox_kernel, tq=tq, tk=tk, nsteps=nsteps),
        grid_spec=grid_spec,
        out_shape=jax.ShapeDtypeStruct((t, width), BF16),
        compiler_params=_params("parallel", "arbitrary"),
        name="fox_attention",
    )(i_tab, j_tab, kind, first, qp, kp, vt)


def _mix_out_kernel(h_ref, main_ref, qm_ref, kbd_ref, vbd_ref, wo_ref, o_ref, *, main_width, mem_len):
    s = _dot(qm_ref[...], kbd_ref[0]) * (MEM_HEAD_DIM ** -0.5)
    probs = []
    for g in range(MEM_HEADS):
        sg = s[:, g * mem_len:(g + 1) * mem_len]
        p = jnp.exp(sg - jnp.max(sg, axis=1, keepdims=True))
        probs.append((p / jnp.sum(p, axis=1, keepdims=True)).astype(BF16))
    mem_out = _dot(jnp.concatenate(probs, axis=1), vbd_ref[0])
    o_ref[...] = (h_ref[...] + _dot(main_ref[...], wo_ref[:main_width, :])
                  + _dot(mem_out.astype(BF16), wo_ref[main_width:, :]))


def mix_out(h, main, qmem, kbd, vbd, wo, layer, batch, tm):
    t, d = h.shape
    main_width = main.shape[1]
    mem_len = kbd.shape[2] // MEM_HEADS
    per_batch = t // batch // tm
    tok = lambda i: (i, 0)
    return pl.pallas_call(
        functools.partial(_mix_out_kernel, main_width=main_width, mem_len=mem_len),
        grid=(t // tm,),
        in_specs=[pl.BlockSpec((tm, d), tok),
                  pl.BlockSpec((tm, main_width), tok),
                  pl.BlockSpec((tm, MEM_WIDTH), tok),
                  pl.BlockSpec((1,) + kbd.shape[1:], lambda i: (i // per_batch, 0, 0)),
                  pl.BlockSpec((1,) + vbd.shape[1:], lambda i: (i // per_batch, 0, 0)),
                  pl.BlockSpec((None,) + wo.shape[1:], lambda i: (layer, 0, 0))],
        out_specs=pl.BlockSpec((tm, d), tok),
        out_shape=jax.ShapeDtypeStruct((t, d), F32),
        compiler_params=_params("parallel"),
        name="mix_out",
    )(h, main, qmem, kbd, vbd, wo)


def _pad_cols(w, n):
    return jnp.pad(w, ((0, 0), (0, n - w.shape[1])))


def _lane_row(vec, offset=0):
    return jnp.zeros((1, LANES), F32).at[0, offset:offset + vec.shape[0]].set(vec.astype(F32))


def _memory_block_diag(kv, batch, mem_len):
    kv = kv.reshape(batch, mem_len, 2 * MEM_WIDTH)
    k, v = kv[..., :MEM_WIDTH], kv[..., MEM_WIDTH:]
    head_of_feat = jnp.arange(MEM_WIDTH) // MEM_HEAD_DIM
    head_of_slot = jnp.arange(MEM_HEADS * mem_len) // mem_len
    mask = head_of_feat[:, None] == head_of_slot[None, :]
    kbd = jnp.where(mask[None], jnp.tile(k.transpose(0, 2, 1), (1, 1, MEM_HEADS)), 0).astype(BF16)
    vbd = jnp.where(mask.T[None], jnp.tile(v, (1, MEM_HEADS, 1)), 0).astype(BF16)
    return kbd, vbd


def kernel(x, mem, ffn1_norm, ffn1_w_gate_up, ffn1_w_down, mix_norm, ffn2_norm, ffn2_w_gate_up,
           ffn2_w_down, gdn_w_in, gdn_conv, gdn_A_log, gdn_dt_bias, gdn_out_norm, fox_w_in, w_out,
           mem_norm, mem_w_kv, kv_norm, kv_w, kv_b_f, final_norm):
    bsz, seq, d = x.shape
    mem_len = mem.shape[1]
    depth = ffn1_norm.shape[0]
    n_a = gdn_w_in.shape[0]
    t = bsz * seq
    tm = min(512, seq)
    tf = 256
    tm_ffn = min(1024, seq)
    tl = min(256, seq)
    tq = min(512, seq)

    h = x.reshape(t, d).astype(F32)

    w_kv_all = jnp.concatenate([mem_w_kv[l] for l in range(depth)], axis=1).astype(BF16)
    (kv_mem,) = norm_matmul(mem.reshape(bsz * mem_len, d).astype(F32), mem_norm, w_kv_all,
                            [(0, w_kv_all.shape[1], False)], [F32], min(256, bsz * mem_len), "mem_kv")

    o0, o1 = 3 * GDN_WIDTH, 4 * GDN_WIDTH
    kp = vt = cfm = None
    wgu1, wd1 = ffn1_w_gate_up.astype(BF16), ffn1_w_down.astype(BF16)
    wgu2, wd2 = ffn2_w_gate_up.astype(BF16), ffn2_w_down.astype(BF16)
    wo_all = w_out.astype(BF16)
    for l in range(depth):
        last = l == depth - 1
        h = ffn(h, ffn1_norm[l], wgu1, wd1, l, final_norm, tm_ffn, tf, False, "ffn1")
        kbd, vbd = _memory_block_diag(kv_mem[:, 2 * MEM_WIDTH * l:2 * MEM_WIDTH * (l + 1)], bsz, mem_len)
        if l < n_a:
            qkv, z, ab, qmem = gdn_in(h, mix_norm[l], gdn_w_in.astype(F32), l, tm)
            alog_row = _lane_row(gdn_A_log[l])
            dtb_row = _lane_row(gdn_dt_bias[l])
            main = gdn_core(qkv, ab, z, gdn_conv[l].astype(F32), alog_row, dtb_row,
                            gdn_out_norm[l].astype(F32), seq, tl)
        else:
            qp, qmem, bnd = fox_in(h, mix_norm[l], fox_w_in[l - n_a].astype(BF16), cfm, kmax, bsz, tq)
            main = fox_attention(qp, kp, vt, bnd, cfm, bsz, tq)
        h = mix_out(h, main, qmem, kbd, vbd, wo_all, l, bsz, tm)
        h = ffn(h, ffn2_norm[l], wgu2, wd2, l, final_norm, tm_ffn, tf, last, "ffn2")
        if l == n_a - 1:
            w_cat = jnp.concatenate([kv_w[:, :2 * FOX_WIDTH], _pad_cols(kv_w[:, 2 * FOX_WIDTH:], LANES)],
                                    axis=1).astype(BF16)
            k_sh, vt, flog = norm_matmul(
                h, kv_norm, w_cat,
                [(0, FOX_WIDTH, False), (FOX_WIDTH, 2 * FOX_WIDTH, True),
                 (2 * FOX_WIDTH, 2 * FOX_WIDTH + LANES, False)],
                [BF16, BF16, F32], tm, "shared_kv")
            cfm, kp, kn2 = forget_cumsum(flog, _lane_row(kv_b_f), k_sh, bsz, tq)
            kmax = _prefix_max_key_norm(kn2)
    return h.reshape(bsz, seq, d)
```

```python
import functools

import jax
import jax.numpy as jnp
from jax import lax
from jax.experimental import pallas as pl
from jax.experimental.pallas import tpu as pltpu

F32 = jnp.float32
BF16 = jnp.bfloat16

EPS = 1e-6
NEG_INF = -1e30
HEAD_DIM = 128
GDN_HEADS = 6
GDN_WIDTH = GDN_HEADS * HEAD_DIM
FOX_HEADS = 6
FOX_WIDTH = FOX_HEADS * HEAD_DIM
CONV_WIDTH = 4
MEM_HEADS = 4
MEM_HEAD_DIM = 64
MEM_WIDTH = MEM_HEADS * MEM_HEAD_DIM

LANES = 128
SUBLANES = 8
GDN_CHUNK = 128
FOX_AUG = 16
LOG2E = 1.4426950408889634
FOX_SKIP_LOG2 = 160.0
VMEM_LIMIT = 56 * 1024 * 1024


def _params(*sem):
    return pltpu.CompilerParams(dimension_semantics=sem, vmem_limit_bytes=VMEM_LIMIT)


def _dot(a, b):
    return jnp.dot(a, b, preferred_element_type=F32)


def _dot_nt(a, b):
    return lax.dot_general(a, b, (((1,), (1,)), ((), ())), preferred_element_type=F32)


def _rms(x, gain):
    return x * lax.rsqrt(jnp.mean(x * x, axis=-1, keepdims=True) + EPS) * gain


def _split3(x):
    hi = x.astype(BF16)
    r1 = x - hi.astype(F32)
    mid = r1.astype(BF16)
    lo = (r1 - mid.astype(F32)).astype(BF16)
    return hi, mid, lo


def _tri_cumsum(tri, x):
    hi, mid, lo = _split3(x)
    return _dot(tri, hi) + _dot(tri, mid) + _dot(tri, lo)


def _norm_matmul_kernel(x_ref, g_ref, w_ref, *out_refs, splits):
    xn = _rms(x_ref[...], g_ref[...]).astype(BF16)
    ys = [_dot(xn, w_ref[:, lo:hi]) for lo, hi, _ in splits]
    for (lo, hi, transposed), o_ref, y in zip(splits, out_refs, ys):
        if transposed:
            o_ref[...] = y.T.astype(o_ref.dtype)
        else:
            o_ref[...] = y.astype(o_ref.dtype)


def norm_matmul(x, gain, w, splits, dtypes, tm, name):
    t, d = x.shape
    n = w.shape[1]
    out_shape, out_specs = [], []
    for (lo, hi, transposed), dt in zip(splits, dtypes):
        if transposed:
            out_shape.append(jax.ShapeDtypeStruct((hi - lo, t), dt))
            out_specs.append(pl.BlockSpec((hi - lo, tm), lambda i: (0, i)))
        else:
            out_shape.append(jax.ShapeDtypeStruct((t, hi - lo), dt))
            out_specs.append(pl.BlockSpec((tm, hi - lo), lambda i: (i, 0)))
    return pl.pallas_call(
        functools.partial(_norm_matmul_kernel, splits=tuple(splits)),
        grid=(t // tm,),
        in_specs=[pl.BlockSpec((tm, d), lambda i: (i, 0)),
                  pl.BlockSpec((1, d), lambda i: (0, 0)),
                  pl.BlockSpec((d, n), lambda i: (0, 0))],
        out_specs=out_specs,
        out_shape=out_shape,
        compiler_params=_params("parallel"),
        name=name,
    )(x, gain.reshape(1, d), w)


def _ffn_kernel(h_ref, g_ref, wgu_ref, wd_ref, fg_ref, o_ref, act_ref, *, hidden, tf, final_norm):
    x = h_ref[...]
    xn = _rms(x, g_ref[...]).astype(BF16)
    for j in range(hidden // tf):
        gate = _dot(xn, wgu_ref[:, j * tf:(j + 1) * tf])
        up = _dot(xn, wgu_ref[:, hidden + j * tf:hidden + (j + 1) * tf])
        act_ref[:, j * tf:(j + 1) * tf] = (gate * jax.nn.sigmoid(gate) * up).astype(BF16)
    out = x + 0.5 * _dot(act_ref[...], wd_ref[...])
    if final_norm:
        out = _rms(out, fg_ref[...])
    o_ref[...] = out


def ffn(h, gain, wgu, wd, layer, final_gain, tm, tf, final_norm, name):
    t, d = h.shape
    hidden = wd.shape[1]
    return pl.pallas_call(
        functools.partial(_ffn_kernel, hidden=hidden, tf=tf, final_norm=final_norm),
        grid=(t // tm,),
        in_specs=[pl.BlockSpec((tm, d), lambda i: (i, 0)),
                  pl.BlockSpec((1, d), lambda i: (0, 0)),
                  pl.BlockSpec((None, d, 2 * hidden), lambda i: (layer, 0, 0), pipeline_mode=pl.Buffered(1)),
                  pl.BlockSpec((None, hidden, d), lambda i: (layer, 0, 0), pipeline_mode=pl.Buffered(1)),
                  pl.BlockSpec((1, d), lambda i: (0, 0))],
        out_specs=pl.BlockSpec((tm, d), lambda i: (i, 0)),
        out_shape=jax.ShapeDtypeStruct((t, d), F32),
        scratch_shapes=[pltpu.VMEM((tm, hidden), BF16)],
        compiler_params=_params("parallel"),
        name=name,
    )(h, gain.reshape(1, d), wgu, wd, final_gain.reshape(1, d))


def _unit_lower_inverses(mats):
    n = mats[0].shape[0]
    row = lax.broadcasted_iota(jnp.int32, (n, n), 0)
    col = lax.broadcasted_iota(jnp.int32, (n, n), 1)
    eye = (row == col).astype(F32)
    pair = (row >> 1) == (col >> 1)
    ts = [eye - jnp.where(pair, a, 0.0) for a in mats]
    s = 2
    while s < n:
        k = s.bit_length() - 1
        sub = ((row >> (k + 1)) == (col >> (k + 1))) & ((row >> k) != (col >> k))
        tbs = [t.astype(BF16) for t in ts]
        tes = [_dot(tb, jnp.where(sub, a, 0.0).astype(BF16)).astype(BF16) for tb, a in zip(tbs, mats)]
        ts = [t - _dot(te, tb) for t, te, tb in zip(ts, tes, tbs)]
        s *= 2
    return ts


def _conv_silu(x, prev, cw):
    ln = x.shape[0]
    xs = jnp.concatenate([prev, x], axis=0)
    y = cw[CONV_WIDTH - 1:CONV_WIDTH, :] * x
    for tap in range(CONV_WIDTH - 1):
        shift = CONV_WIDTH - 1 - tap
        y = y + cw[tap:tap + 1, :] * xs[SUBLANES - shift:SUBLANES - shift + ln, :]
    return y * jax.nn.sigmoid(y)


def _gdn_in_kernel(x_ref, g_ref, w_ref, qkv_ref, z_ref, ab_ref, qm_ref, wb_ref):
    @pl.when(pl.program_id(0) == 0)
    def _():
        wb_ref[...] = w_ref[...].astype(BF16)

    o0, o1 = 3 * GDN_WIDTH, 4 * GDN_WIDTH
    ab_off = 2 * GDN_HEADS
    xn = _rms(x_ref[...], g_ref[...]).astype(BF16)
    qkv_ref[...] = _dot(xn, wb_ref[:, :o0])
    z_ref[...] = _dot(xn, wb_ref[:, o0:o1])
    rest = _dot(xn, wb_ref[:, o1:])
    ab_ref[...] = rest[:, :LANES]
    qm_ref[...] = rest[:, ab_off:ab_off + MEM_WIDTH].astype(qm_ref.dtype)


def gdn_in(h, gain, w_in, layer, tm):
    t, d = h.shape
    n = w_in.shape[2]
    w3 = 3 * GDN_WIDTH
    tok = lambda i: (i, 0)
    const = lambda i: (0, 0)
    return pl.pallas_call(
        _gdn_in_kernel,
        grid=(t // tm,),
        in_specs=[pl.BlockSpec((tm, d), tok),
                  pl.BlockSpec((1, d), const),
                  pl.BlockSpec((None, d, n), lambda i: (layer, 0, 0), pipeline_mode=pl.Buffered(1))],
        out_specs=[pl.BlockSpec((tm, w3), tok),
                   pl.BlockSpec((tm, GDN_WIDTH), tok),
                   pl.BlockSpec((tm, LANES), tok),
                   pl.BlockSpec((tm, MEM_WIDTH), tok)],
        out_shape=[jax.ShapeDtypeStruct((t, w3), F32),
                   jax.ShapeDtypeStruct((t, GDN_WIDTH), F32),
                   jax.ShapeDtypeStruct((t, LANES), F32),
                   jax.ShapeDtypeStruct((t, MEM_WIDTH), BF16)],
        scratch_shapes=[pltpu.VMEM((d, n), BF16)],
        compiler_params=_params("arbitrary"),
        name="gdn_in",
    )(h, gain.reshape(1, d), w_in)


def _gdn_core_kernel(x_ref, xp_ref, ab_ref, z_ref, cw_ref, alog_ref, dtb_ref, on_ref, o_ref, s_ref,
                     *, blocks_per_seq):
    c = GDN_CHUNK
    nchunk = x_ref.shape[0] // c
    first = (pl.program_id(0) % blocks_per_seq) == 0

    @pl.when(first)
    def _():
        s_ref[...] = jnp.zeros_like(s_ref)

    prev = jnp.where(first, 0.0, xp_ref[...])
    y = _conv_silu(x_ref[...], prev, cw_ref[...])

    ab = ab_ref[...]
    pre = ab + dtb_ref[...]
    softplus = jnp.maximum(pre, 0.0) + jnp.log(1.0 + jnp.exp(-jnp.abs(pre)))
    g_col = -jnp.exp(alog_ref[...]) * softplus
    beta_col = jax.nn.sigmoid(ab)

    row = lax.broadcasted_iota(jnp.int32, (c, c), 0)
    col = lax.broadcasted_iota(jnp.int32, (c, c), 1)
    lower = row >= col
    tri = lower.astype(BF16)

    mats, rhss, egs, qhs, qks, kts = [], [], [], [], [], []
    for ci in range(nchunk):
        r0 = ci * c
        gc_col = _tri_cumsum(tri, g_col[r0:r0 + c, :])
        gc_row = gc_col.T
        egs.append(jnp.exp(jnp.broadcast_to(gc_row[0:2 * SUBLANES, c - 1:c], (2 * SUBLANES, LANES))))
        for h in range(GDN_HEADS):
            q = y[r0:r0 + c, h * HEAD_DIM:(h + 1) * HEAD_DIM]
            k = y[r0:r0 + c, GDN_WIDTH + h * HEAD_DIM:GDN_WIDTH + (h + 1) * HEAD_DIM]
            v = y[r0:r0 + c, 2 * GDN_WIDTH + h * HEAD_DIM:2 * GDN_WIDTH + (h + 1) * HEAD_DIM]
            q = q * lax.rsqrt(jnp.sum(q * q, axis=-1, keepdims=True) + EPS) * (HEAD_DIM ** -0.5)
            k = k * lax.rsqrt(jnp.sum(k * k, axis=-1, keepdims=True) + EPS)
            beta = beta_col[r0:r0 + c, GDN_HEADS + h:GDN_HEADS + h + 1]
            gcl = gc_col[:, h:h + 1]
            gr = gc_row[h:h + 1, :]
            glast = gr[:, c - 1:c]
            decay = jnp.where(lower, jnp.exp(jnp.where(lower, gcl - gr, 0.0)), 0.0)
            kb = k * beta
            kbf = k.astype(BF16)
            mats.append(jnp.where(row > col, _dot_nt(kb.astype(BF16), kbf) * decay, 0.0))
            qk = _dot_nt(q.astype(BF16), kbf) * decay
            egc = jnp.exp(gcl)
            rhss.append(jnp.concatenate([v * beta, kb * egc], axis=1).astype(BF16))
            qhs.append((q * egc).astype(BF16))
            qks.append(qk.astype(BF16))
            kts.append((k * jnp.exp(glast - gcl)).T.astype(BF16))

    gain = on_ref[...]

    def recurrence_stages(ci, sols):
        rows = slice(ci * c, (ci + 1) * c)
        eg = egs[ci]
        probs = range(ci * GDN_HEADS, (ci + 1) * GDN_HEADS)
        live = {}

        def project():
            live["states"] = [s_ref[h] for h in range(GDN_HEADS)]
            live["sbs"] = [s.astype(BF16) for s in live["states"]]
            live["ws"] = [_dot(sol[:, HEAD_DIM:].astype(BF16), sb) for sol, sb in zip(sols, live["sbs"])]

        def residual():
            live["vbs"] = [(sol[:, :HEAD_DIM] - w).astype(BF16) for sol, w in zip(sols, live["ws"])]

        def advance():
            for h, (p, state, vb) in enumerate(zip(probs, live["states"], live["vbs"])):
                s_ref[h] = state * eg[h:h + 1, :] + _dot(kts[p], vb)

        def emit():
            outs = [_dot(qhs[p], sb) + _dot(qks[p], vb) for p, sb, vb in zip(probs, live["sbs"], live["vbs"])]
            for h, out in enumerate(outs):
                cs = slice(h * HEAD_DIM, (h + 1) * HEAD_DIM)
                zz = z_ref[rows, cs]
                o_ref[rows, cs] = (_rms(out, gain) * (zz * jax.nn.sigmoid(zz))).astype(o_ref.dtype)

        return (project, residual, advance), emit

    sols = [_dot(t.astype(BF16), rhs) for t, rhs in zip(_unit_lower_inverses(mats), rhss)]
    for ci in range(nchunk):
        chain, emit = recurrence_stages(ci, sols[ci * GDN_HEADS:(ci + 1) * GDN_HEADS])
        for stage in chain:
            stage()
        emit()


def gdn_core(qkv, ab, z, conv_w, alog_row, dtb_row, out_norm, seq, tl):
    t = qkv.shape[0]
    w3 = qkv.shape[1]
    width = w3 // 3
    nblk = t // tl
    prev_blocks = tl // SUBLANES
    tok = lambda i: (i, 0)
    const = lambda i: (0, 0)
    return pl.pallas_call(
        functools.partial(_gdn_core_kernel, blocks_per_seq=seq // tl),
        grid=(nblk,),
        in_specs=[pl.BlockSpec((tl, w3), tok),
                  pl.BlockSpec((SUBLANES, w3), lambda i: (jnp.maximum(i * prev_blocks - 1, 0), 0)),
                  pl.BlockSpec((tl, LANES), tok),
                  pl.BlockSpec((tl, width), tok),
                  pl.BlockSpec((CONV_WIDTH, w3), const),
                  pl.BlockSpec((1, LANES), const),
                  pl.BlockSpec((1, LANES), const),
                  pl.BlockSpec((1, HEAD_DIM), const)],
        out_specs=pl.BlockSpec((tl, width), tok),
        out_shape=jax.ShapeDtypeStruct((t, width), BF16),
        scratch_shapes=[pltpu.VMEM((GDN_HEADS, HEAD_DIM, HEAD_DIM), F32)],
        compiler_params=_params("arbitrary"),
        name="gdn_core",
    )(qkv, qkv, ab, z, conv_w, alog_row, dtb_row, out_norm.reshape(1, HEAD_DIM))


def _forget_cumsum_kernel(f_ref, b_ref, k_ref, cfm_ref, kp_ref, kn_ref, carry_ref):
    @pl.when(pl.program_id(1) == 0)
    def _():
        carry_ref[...] = jnp.zeros_like(carry_ref)

    n = f_ref.shape[0]
    pre = f_ref[...] + b_ref[...]
    logf = jnp.minimum(pre, 0.0) - jnp.log(1.0 + jnp.exp(-jnp.abs(pre)))
    row = lax.broadcasted_iota(jnp.int32, (n, n), 0)
    col = lax.broadcasted_iota(jnp.int32, (n, n), 1)
    cs = _tri_cumsum((row >= col).astype(BF16), logf) + carry_ref[...]
    carry_ref[...] = cs[n - 1:n, :]
    cfm_ref[0] = cs.T[:SUBLANES, :]
    hi, mid, lo = (p.astype(F32) for p in _split3(-LOG2E * cs))
    lane = lax.broadcasted_iota(jnp.int32, (n, LANES), 1)
    norms = []
    for h in range(FOX_HEADS):
        aug = jnp.where(lane < 3, 1.0,
                        jnp.where(lane == 3, hi[:, h:h + 1],
                                  jnp.where(lane == 4, mid[:, h:h + 1],
                                            jnp.where(lane == 5, lo[:, h:h + 1], 0.0))))
        kh = k_ref[:, h * HEAD_DIM:(h + 1) * HEAD_DIM]
        kp_ref[h, :, :HEAD_DIM] = kh
        kp_ref[h, :, HEAD_DIM:] = aug[:, :FOX_AUG].astype(BF16)
        kf = kh.astype(F32)
        kn2 = jnp.max(jnp.sum(kf * kf, axis=1, keepdims=True), axis=0, keepdims=True)
        norms.append(jnp.broadcast_to(kn2, (1, LANES)))
    kn_ref[0, 0] = jnp.concatenate(norms + [jnp.zeros((SUBLANES - FOX_HEADS, LANES), F32)], axis=0)


def forget_cumsum(flog, bias_row, k, batch, tl):
    t, width = k.shape
    seq = t // batch
    nblk = seq // tl
    tok = lambda b, i: (b * nblk + i, 0)
    return pl.pallas_call(
        _forget_cumsum_kernel,
        grid=(batch, nblk),
        in_specs=[pl.BlockSpec((tl, LANES), tok),
                  pl.BlockSpec((1, LANES), lambda b, i: (0, 0)),
                  pl.BlockSpec((tl, width), tok)],
        out_specs=[pl.BlockSpec((1, SUBLANES, tl), lambda b, i: (b, 0, i)),
                   pl.BlockSpec((FOX_HEADS, tl, HEAD_DIM + FOX_AUG), lambda b, i: (0, b * nblk + i, 0)),
                   pl.BlockSpec((1, 1, SUBLANES, LANES), lambda b, i: (b, i, 0, 0))],
        out_shape=[jax.ShapeDtypeStruct((batch, SUBLANES, seq), F32),
                   jax.ShapeDtypeStruct((FOX_HEADS, t, HEAD_DIM + FOX_AUG), BF16),
                   jax.ShapeDtypeStruct((batch, nblk, SUBLANES, LANES), F32)],
        scratch_shapes=[pltpu.VMEM((1, LANES), F32)],
        compiler_params=_params("parallel", "arbitrary"),
        name="forget_cumsum",
    )(flog, bias_row, k)


def _fox_in_kernel(x_ref, g_ref, w_ref, cfm_ref, kmax_ref, qp_ref, qm_ref, bnd_ref):
    xn = _rms(x_ref[...], g_ref[...]).astype(BF16)
    q = _dot(xn, w_ref[:, :FOX_WIDTH]) * (LOG2E * HEAD_DIM ** -0.5)
    qm_ref[...] = _dot(xn, w_ref[:, FOX_WIDTH:]).astype(qm_ref.dtype)
    qt = q.T
    tm = qt.shape[1]
    c2 = LOG2E * cfm_ref[0]
    hi, mid, lo = (p.astype(F32) for p in _split3(c2))
    kmax = kmax_ref[0, 0]
    r = lax.broadcasted_iota(jnp.int32, (FOX_AUG, tm), 0)
    bnds = []
    for h in range(FOX_HEADS):
        aug = jnp.where(r == 0, hi[h:h + 1, :],
                        jnp.where(r == 1, mid[h:h + 1, :],
                                  jnp.where(r == 2, lo[h:h + 1, :],
                                            jnp.where(r < 6, 1.0, 0.0))))
        qb = qt[h * HEAD_DIM:(h + 1) * HEAD_DIM, :].astype(BF16)
        qp_ref[h, :HEAD_DIM, :] = qb
        qp_ref[h, HEAD_DIM:, :] = aug.astype(BF16)
        qf = qb.astype(F32)
        q_norm = jnp.sqrt(jnp.sum(qf * qf, axis=0, keepdims=True))
        worst = jnp.max(2.0 * q_norm * kmax[h:h + 1, :1] + c2[h:h + 1, :], axis=1, keepdims=True)
        bnds.append(jnp.broadcast_to(worst + FOX_SKIP_LOG2, (1, LANES)))
    bnd_ref[0, 0] = jnp.concatenate(bnds + [jnp.zeros((SUBLANES - FOX_HEADS, LANES), F32)], axis=0)


def fox_in(x, gain, w, cfm, kmax, batch, tm):
    t, d = x.shape
    per_batch = t // batch // tm
    assert kmax.shape[1] == per_batch
    blk = lambda i: (i // per_batch, i % per_batch, 0, 0)
    return pl.pallas_call(
        _fox_in_kernel,
        grid=(t // tm,),
        in_specs=[pl.BlockSpec((tm, d), lambda i: (i, 0)),
                  pl.BlockSpec((1, d), lambda i: (0, 0)),
                  pl.BlockSpec(w.shape, lambda i: (0, 0)),
                  pl.BlockSpec((1, SUBLANES, tm), lambda i: (i // per_batch, 0, i % per_batch)),
                  pl.BlockSpec((1, 1, SUBLANES, LANES), blk)],
        out_specs=[pl.BlockSpec((FOX_HEADS, HEAD_DIM + FOX_AUG, tm), lambda i: (0, 0, i)),
                   pl.BlockSpec((tm, MEM_WIDTH), lambda i: (i, 0)),
                   pl.BlockSpec((1, 1, SUBLANES, LANES), blk)],
        out_shape=[jax.ShapeDtypeStruct((FOX_HEADS, HEAD_DIM + FOX_AUG, t), BF16),
                   jax.ShapeDtypeStruct((t, MEM_WIDTH), BF16),
                   jax.ShapeDtypeStruct((batch, per_batch, SUBLANES, LANES), F32)],
        compiler_params=_params("parallel"),
        name="fox_in",
    )(x, gain.reshape(1, d), w, cfm, kmax)


def _fox_kernel(it_ref, jt_ref, kind_ref, first_ref, qp_ref, kp_ref, vt_ref, o_ref, m_ref, acc_ref,
                *, tq, tk, nsteps):
    del it_ref, jt_ref
    step = pl.program_id(0) * nsteps + pl.program_id(1)
    kind = kind_ref[step]

    @pl.when(first_ref[step] == 1)
    def _():
        m_ref[...] = jnp.full_like(m_ref, NEG_INF)
        acc_ref[...] = jnp.zeros_like(acc_ref)

    ones_rows = (lax.broadcasted_iota(jnp.int32, (FOX_AUG, tk), 0) == 0).astype(BF16)

    def heads(masked):
        if masked:
            causal = (lax.broadcasted_iota(jnp.int32, (tk, tq), 0)
                      <= lax.broadcasted_iota(jnp.int32, (tk, tq), 1))
        ahead = 2
        scores = [_dot(kp_ref[h], qp_ref[h]) for h in range(ahead)]
        for h in range(FOX_HEADS):
            s = scores[h]
            if h + ahead < FOX_HEADS:
                scores.append(_dot(kp_ref[h + ahead], qp_ref[h + ahead]))
            if masked:
                s = jnp.where(causal, s, NEG_INF)
            m_prev = m_ref[h:h + 1, :]
            m_next = jnp.maximum(m_prev, jnp.max(s, axis=0, keepdims=True))
            alpha = jnp.exp2(m_prev - m_next)
            p = jnp.exp2(s - m_next).astype(BF16)
            vt = jnp.concatenate([vt_ref[h * HEAD_DIM:(h + 1) * HEAD_DIM, :], ones_rows], axis=0)
            acc_ref[h] = alpha * acc_ref[h] + _dot(vt, p)
            m_ref[h:h + 1, :] = m_next

    @pl.when(kind == 1)
    def _():
        heads(False)

    @pl.when(kind == 2)
    def _():
        heads(True)
        for h in range(FOX_HEADS):
            acc = acc_ref[h]
            out = acc[:HEAD_DIM, :] / acc[HEAD_DIM:HEAD_DIM + 1, :]
            o_ref[:, h * HEAD_DIM:(h + 1) * HEAD_DIM] = out.T.astype(o_ref.dtype)


def _prefix_max_key_norm(kn2):
    nblk = kn2.shape[1]
    blk = jnp.arange(nblk)
    upto = (blk[None, :] <= blk[:, None])[None, :, :, None, None]
    return jnp.sqrt(jnp.max(jnp.where(upto, kn2[:, None], 0.0), axis=2))


def _fox_step_tables(bnd, cfm, nq, tq):
    nsteps = nq * (nq + 1) // 2
    blk = jnp.arange(nq, dtype=jnp.int32)
    bnd = bnd[:, :, :FOX_HEADS, 0]
    cend = (LOG2E * cfm[:, :FOX_HEADS, tq - 1::tq]).transpose(0, 2, 1)
    negligible = jnp.all(cend[:, None, :, :] >= bnd[:, :, None, :], axis=-1)
    needed = (~negligible | (blk[None, :] >= blk[:, None])[None]).astype(jnp.int32)
    j_first = jnp.sum((jnp.cumsum(needed, axis=2) == 0).astype(jnp.int32), axis=2)
    count = blk[None, :] + 1 - j_first
    end = jnp.cumsum(count, axis=1)
    start = end - count
    slot = jnp.arange(nsteps, dtype=jnp.int32)[None, :, None]
    it = jnp.minimum(jnp.sum((slot >= end[:, None, :]).astype(jnp.int32), axis=2), nq - 1)
    mine = (it[:, :, None] == blk[None, None, :]).astype(jnp.int32)
    jt = jnp.minimum(jnp.sum(mine * (j_first - start)[:, None, :], axis=2) + slot[:, :, 0], it)
    valid = (slot[:, :, 0] < end[:, nq - 1:nq]).astype(jnp.int32)
    kind = valid * (1 + (jt == it).astype(jnp.int32))
    first = valid * jnp.sum(mine * (start[:, None, :] == slot).astype(jnp.int32), axis=2)
    return it.reshape(-1), jt.reshape(-1), kind.reshape(-1), first.reshape(-1), nsteps


def fox_attention(qp, kp, vt, bnd, cfm, batch, tq):
    heads, kdim, t = qp.shape
    width = vt.shape[0]
    seq = t // batch
    nq = seq // tq
    tk = tq
    i_tab, j_tab, kind, first, nsteps = _fox_step_tables(bnd, cfm, nq, tq)
    qblk = lambda b, s, it, jt, kd, fs: b * nq + it[b * nsteps + s]
    kblk = lambda b, s, it, jt, kd, fs: b * nq + jt[b * nsteps + s]
    grid_spec = pltpu.PrefetchScalarGridSpec(
        num_scalar_prefetch=4,
        grid=(batch, nsteps),
        in_specs=[pl.BlockSpec((heads, kdim, tq), lambda *a: (0, 0, qblk(*a))),
                  pl.BlockSpec((heads, tk, kdim), lambda *a: (0, kblk(*a), 0)),
                  pl.BlockSpec((width, tk), lambda *a: (0, kblk(*a)))],
        out_specs=pl.BlockSpec((tq, width), lambda *a: (qblk(*a), 0)),
        scratch_shapes=[pltpu.VMEM((SUBLANES, tq), F32),
                        pltpu.VMEM((heads, kdim, tq), F32)])
    return pl.pallas_call(
        functools.partial(_fox_kernel, tq=tq, tk=tk, nsteps=nsteps),
        grid_spec=grid_spec,
        out_shape=jax.ShapeDtypeStruct((t, width), BF16),
        compiler_params=_params("parallel", "arbitrary"),
        name="fox_attention",
    )(i_tab, j_tab, kind, first, qp, kp, vt)


def _mix_out_kernel(h_ref, main_ref, qm_ref, kbd_ref, vbd_ref, wo_ref, o_ref, *, main_width, mem_len):
    s = _dot(qm_ref[...], kbd_ref[0]) * (MEM_HEAD_DIM ** -0.5)
    acc = h_ref[...] + _dot(main_ref[...], wo_ref[:main_width, :])
    probs = []
    for g in range(MEM_HEADS):
        sg = s[:, g * mem_len:(g + 1) * mem_len]
        p = jnp.exp(sg - jnp.max(sg, axis=1, keepdims=True))
        probs.append((p / jnp.sum(p, axis=1, keepdims=True)).astype(BF16))
    mem_out = _dot(jnp.concatenate(probs, axis=1), vbd_ref[0])
    o_ref[...] = acc + _dot(mem_out.astype(BF16), wo_ref[main_width:, :])


def mix_out(h, main, qmem, kbd, vbd, wo, layer, batch, tm):
    t, d = h.shape
    main_width = main.shape[1]
    mem_len = kbd.shape[2] // MEM_HEADS
    per_batch = t // batch // tm
    tok = lambda i: (i, 0)
    return pl.pallas_call(
        functools.partial(_mix_out_kernel, main_width=main_width, mem_len=mem_len),
        grid=(t // tm,),
        in_specs=[pl.BlockSpec((tm, d), tok),
                  pl.BlockSpec((tm, main_width), tok),
                  pl.BlockSpec((tm, MEM_WIDTH), tok),
                  pl.BlockSpec((1,) + kbd.shape[1:], lambda i: (i // per_batch, 0, 0)),
                  pl.BlockSpec((1,) + vbd.shape[1:], lambda i: (i // per_batch, 0, 0)),
                  pl.BlockSpec((None,) + wo.shape[1:], lambda i: (layer, 0, 0))],
        out_specs=pl.BlockSpec((tm, d), tok),
        out_shape=jax.ShapeDtypeStruct((t, d), F32),
        compiler_params=_params("parallel"),
        name="mix_out",
    )(h, main, qmem, kbd, vbd, wo)


def _pad_cols(w, n):
    return jnp.pad(w, ((0, 0), (0, n - w.shape[1])))


def _lane_row(vec, offset=0):
    return jnp.zeros((1, LANES), F32).at[0, offset:offset + vec.shape[0]].set(vec.astype(F32))


def _memory_block_diag(kv, batch, mem_len):
    kv = kv.reshape(batch, mem_len, 2 * MEM_WIDTH)
    k, v = kv[..., :MEM_WIDTH], kv[..., MEM_WIDTH:]
    head_of_feat = jnp.arange(MEM_WIDTH) // MEM_HEAD_DIM
    head_of_slot = jnp.arange(MEM_HEADS * mem_len) // mem_len
    mask = head_of_feat[:, None] == head_of_slot[None, :]
    kbd = jnp.where(mask[None], jnp.tile(k.transpose(0, 2, 1), (1, 1, MEM_HEADS)), 0).astype(BF16)
    vbd = jnp.where(mask.T[None], jnp.tile(v, (1, MEM_HEADS, 1)), 0).astype(BF16)
    return kbd, vbd


def kernel(x, mem, ffn1_norm, ffn1_w_gate_up, ffn1_w_down, mix_norm, ffn2_norm, ffn2_w_gate_up,
           ffn2_w_down, gdn_w_in, gdn_conv, gdn_A_log, gdn_dt_bias, gdn_out_norm, fox_w_in, w_out,
           mem_norm, mem_w_kv, kv_norm, kv_w, kv_b_f, final_norm):
    bsz, seq, d = x.shape
    mem_len = mem.shape[1]
    depth = ffn1_norm.shape[0]
    n_a = gdn_w_in.shape[0]
    t = bsz * seq
    tm = min(512, seq)
    tf = 256
    tm_ffn = min(1024, seq)
    tl = min(256, seq)
    tq = min(512, seq)

    h = x.reshape(t, d).astype(F32)

    w_kv_all = jnp.concatenate([mem_w_kv[l] for l in range(depth)], axis=1).astype(BF16)
    (kv_mem,) = norm_matmul(mem.reshape(bsz * mem_len, d).astype(F32), mem_norm, w_kv_all,
                            [(0, w_kv_all.shape[1], False)], [F32], min(256, bsz * mem_len), "mem_kv")

    o0, o1 = 3 * GDN_WIDTH, 4 * GDN_WIDTH
    kp = vt = cfm = None
    wgu1, wd1 = ffn1_w_gate_up.astype(BF16), ffn1_w_down.astype(BF16)
    wgu2, wd2 = ffn2_w_gate_up.astype(BF16), ffn2_w_down.astype(BF16)
    wo_all = w_out.astype(BF16)
    for l in range(depth):
        last = l == depth - 1
        h = ffn(h, ffn1_norm[l], wgu1, wd1, l, final_norm, tm_ffn, tf, False, "ffn1")
        kbd, vbd = _memory_block_diag(kv_mem[:, 2 * MEM_WIDTH * l:2 * MEM_WIDTH * (l + 1)], bsz, mem_len)
        if l < n_a:
            qkv, z, ab, qmem = gdn_in(h, mix_norm[l], gdn_w_in.astype(F32), l, tm)
            alog_row = _lane_row(gdn_A_log[l])
            dtb_row = _lane_row(gdn_dt_bias[l])
            main = gdn_core(qkv, ab, z, gdn_conv[l].astype(F32), alog_row, dtb_row,
                            gdn_out_norm[l].astype(F32), seq, tl)
        else:
            qp, qmem, bnd = fox_in(h, mix_norm[l], fox_w_in[l - n_a].astype(BF16), cfm, kmax, bsz, tq)
            main = fox_attention(qp, kp, vt, bnd, cfm, bsz, tq)
        h = mix_out(h, main, qmem, kbd, vbd, wo_all, l, bsz, tm)
        h = ffn(h, ffn2_norm[l], wgu2, wd2, l, final_norm, tm_ffn, tf, last, "ffn2")
        if l == n_a - 1:
            w_cat = jnp.concatenate([kv_w[:, :2 * FOX_WIDTH], _pad_cols(kv_w[:, 2 * FOX_WIDTH:], LANES)],
                                    axis=1).astype(BF16)
            vt, k_sh, flog = norm_matmul(
                h, kv_norm, w_cat,
                [(FOX_WIDTH, 2 * FOX_WIDTH, True), (0, FOX_WIDTH, False),
                 (2 * FOX_WIDTH, 2 * FOX_WIDTH + LANES, False)],
                [BF16, BF16, F32], tm, "shared_kv")
            cfm, kp, kn2 = forget_cumsum(flog, _lane_row(kv_b_f), k_sh, bsz, tq)
            kmax = _prefix_max_key_norm(kn2)
    return h.reshape(bsz, seq, d)
```

```python
import functools

import jax
import jax.numpy as jnp
from jax import lax
from jax.experimental import pallas as pl
from jax.experimental.pallas import tpu as pltpu

F32 = jnp.float32
BF16 = jnp.bfloat16

EPS = 1e-6
NEG_INF = -1e30
HEAD_DIM = 128
GDN_HEADS = 6
GDN_WIDTH = GDN_HEADS * HEAD_DIM
FOX_HEADS = 6
FOX_WIDTH = FOX_HEADS * HEAD_DIM
CONV_WIDTH = 4
MEM_HEADS = 4
MEM_HEAD_DIM = 64
MEM_WIDTH = MEM_HEADS * MEM_HEAD_DIM

LANES = 128
SUBLANES = 8
GDN_CHUNK = 128
FOX_AUG = 16
LOG2E = 1.4426950408889634
FOX_SKIP_LOG2 = 160.0
FOX_WINDOW = 4
VMEM_LIMIT = 56 * 1024 * 1024


def _params(*sem):
    return pltpu.CompilerParams(dimension_semantics=sem, vmem_limit_bytes=VMEM_LIMIT)


def _dot(a, b):
    return jnp.dot(a, b, preferred_element_type=F32)


def _dot_nt(a, b):
    return lax.dot_general(a, b, (((1,), (1,)), ((), ())), preferred_element_type=F32)


def _rms(x, gain):
    return x * lax.rsqrt(jnp.mean(x * x, axis=-1, keepdims=True) + EPS) * gain


def _split3(x):
    hi = x.astype(BF16)
    r1 = x - hi.astype(F32)
    mid = r1.astype(BF16)
    lo = (r1 - mid.astype(F32)).astype(BF16)
    return hi, mid, lo


def _tri_cumsum(tri, x):
    hi, mid, lo = _split3(x)
    return _dot(tri, hi) + _dot(tri, mid) + _dot(tri, lo)


def _norm_matmul_kernel(x_ref, g_ref, w_ref, *out_refs, splits):
    xn = _rms(x_ref[...], g_ref[...]).astype(BF16)
    ys = [_dot(xn, w_ref[:, lo:hi]) for lo, hi, _ in splits]
    for (lo, hi, transposed), o_ref, y in zip(splits, out_refs, ys):
        if transposed:
            o_ref[...] = y.T.astype(o_ref.dtype)
        else:
            o_ref[...] = y.astype(o_ref.dtype)


def norm_matmul(x, gain, w, splits, dtypes, tm, name):
    t, d = x.shape
    n = w.shape[1]
    out_shape, out_specs = [], []
    for (lo, hi, transposed), dt in zip(splits, dtypes):
        if transposed:
            out_shape.append(jax.ShapeDtypeStruct((hi - lo, t), dt))
            out_specs.append(pl.BlockSpec((hi - lo, tm), lambda i: (0, i)))
        else:
            out_shape.append(jax.ShapeDtypeStruct((t, hi - lo), dt))
            out_specs.append(pl.BlockSpec((tm, hi - lo), lambda i: (i, 0)))
    return pl.pallas_call(
        functools.partial(_norm_matmul_kernel, splits=tuple(splits)),
        grid=(t // tm,),
        in_specs=[pl.BlockSpec((tm, d), lambda i: (i, 0)),
                  pl.BlockSpec((1, d), lambda i: (0, 0)),
                  pl.BlockSpec((d, n), lambda i: (0, 0))],
        out_specs=out_specs,
        out_shape=out_shape,
        compiler_params=_params("parallel"),
        name=name,
    )(x, gain.reshape(1, d), w)


def _ffn_kernel(h_ref, g_ref, wgu_ref, wd_ref, fg_ref, o_ref, act_ref, *, hidden, tf, final_norm):
    x = h_ref[...]
    xn = _rms(x, g_ref[...]).astype(BF16)
    for j in range(hidden // tf):
        gate = _dot(xn, wgu_ref[:, j * tf:(j + 1) * tf])
        up = _dot(xn, wgu_ref[:, hidden + j * tf:hidden + (j + 1) * tf])
        act_ref[:, j * tf:(j + 1) * tf] = (gate * jax.nn.sigmoid(gate) * up).astype(BF16)
    out = x + 0.5 * _dot(act_ref[...], wd_ref[...])
    if final_norm:
        out = _rms(out, fg_ref[...])
    o_ref[...] = out


def ffn(h, gain, wgu, wd, layer, final_gain, tm, tf, final_norm, name):
    t, d = h.shape
    hidden = wd.shape[1]
    return pl.pallas_call(
        functools.partial(_ffn_kernel, hidden=hidden, tf=tf, final_norm=final_norm),
        grid=(t // tm,),
        in_specs=[pl.BlockSpec((tm, d), lambda i: (i, 0)),
                  pl.BlockSpec((1, d), lambda i: (0, 0)),
                  pl.BlockSpec((None, d, 2 * hidden), lambda i: (layer, 0, 0), pipeline_mode=pl.Buffered(1)),
                  pl.BlockSpec((None, hidden, d), lambda i: (layer, 0, 0), pipeline_mode=pl.Buffered(1)),
                  pl.BlockSpec((1, d), lambda i: (0, 0))],
        out_specs=pl.BlockSpec((tm, d), lambda i: (i, 0)),
        out_shape=jax.ShapeDtypeStruct((t, d), F32),
        scratch_shapes=[pltpu.VMEM((tm, hidden), BF16)],
        compiler_params=_params("parallel"),
        name=name,
    )(h, gain.reshape(1, d), wgu, wd, final_gain.reshape(1, d))


def _unit_lower_inverses(mats):
    n = mats[0].shape[0]
    row = lax.broadcasted_iota(jnp.int32, (n, n), 0)
    col = lax.broadcasted_iota(jnp.int32, (n, n), 1)
    eye = (row == col).astype(F32)
    pair = (row >> 1) == (col >> 1)
    ts = [eye - jnp.where(pair, a, 0.0) for a in mats]
    s = 2
    while s < n:
        k = s.bit_length() - 1
        sub = ((row >> (k + 1)) == (col >> (k + 1))) & ((row >> k) != (col >> k))
        tbs = [t.astype(BF16) for t in ts]
        tes = [_dot(tb, jnp.where(sub, a, 0.0).astype(BF16)).astype(BF16) for tb, a in zip(tbs, mats)]
        ts = [t - _dot(te, tb) for t, te, tb in zip(ts, tes, tbs)]
        s *= 2
    return ts


def _conv_silu(x, prev, cw):
    ln = x.shape[0]
    xs = jnp.concatenate([prev, x], axis=0)
    y = cw[CONV_WIDTH - 1:CONV_WIDTH, :] * x
    for tap in range(CONV_WIDTH - 1):
        shift = CONV_WIDTH - 1 - tap
        y = y + cw[tap:tap + 1, :] * xs[SUBLANES - shift:SUBLANES - shift + ln, :]
    return y * jax.nn.sigmoid(y)


def _gdn_in_kernel(x_ref, g_ref, w_ref, qkv_ref, z_ref, ab_ref, qm_ref, wb_ref):
    @pl.when(pl.program_id(0) == 0)
    def _():
        wb_ref[...] = w_ref[...].astype(BF16)

    o0, o1 = 3 * GDN_WIDTH, 4 * GDN_WIDTH
    ab_off = 2 * GDN_HEADS
    xn = _rms(x_ref[...], g_ref[...]).astype(BF16)
    qkv_ref[...] = _dot(xn, wb_ref[:, :o0])
    z_ref[...] = _dot(xn, wb_ref[:, o0:o1])
    rest = _dot(xn, wb_ref[:, o1:])
    ab_ref[...] = rest[:, :LANES]
    qm_ref[...] = rest[:, ab_off:ab_off + MEM_WIDTH].astype(qm_ref.dtype)


def gdn_in(h, gain, w_in, layer, tm):
    t, d = h.shape
    n = w_in.shape[2]
    w3 = 3 * GDN_WIDTH
    tok = lambda i: (i, 0)
    const = lambda i: (0, 0)
    return pl.pallas_call(
        _gdn_in_kernel,
        grid=(t // tm,),
        in_specs=[pl.BlockSpec((tm, d), tok),
                  pl.BlockSpec((1, d), const),
                  pl.BlockSpec((None, d, n), lambda i: (layer, 0, 0), pipeline_mode=pl.Buffered(1))],
        out_specs=[pl.BlockSpec((tm, w3), tok),
                   pl.BlockSpec((tm, GDN_WIDTH), tok),
                   pl.BlockSpec((tm, LANES), tok),
                   pl.BlockSpec((tm, MEM_WIDTH), tok)],
        out_shape=[jax.ShapeDtypeStruct((t, w3), F32),
                   jax.ShapeDtypeStruct((t, GDN_WIDTH), F32),
                   jax.ShapeDtypeStruct((t, LANES), F32),
                   jax.ShapeDtypeStruct((t, MEM_WIDTH), BF16)],
        scratch_shapes=[pltpu.VMEM((d, n), BF16)],
        compiler_params=_params("arbitrary"),
        name="gdn_in",
    )(h, gain.reshape(1, d), w_in)


def _gdn_core_kernel(x_ref, xp_ref, ab_ref, z_ref, cw_ref, alog_ref, dtb_ref, on_ref, o_ref, s_ref,
                     *, blocks_per_seq):
    c = GDN_CHUNK
    nchunk = x_ref.shape[0] // c
    first = (pl.program_id(0) % blocks_per_seq) == 0

    @pl.when(first)
    def _():
        s_ref[...] = jnp.zeros_like(s_ref)

    prev = jnp.where(first, 0.0, xp_ref[...])
    y = _conv_silu(x_ref[...], prev, cw_ref[...])

    ab = ab_ref[...]
    pre = ab + dtb_ref[...]
    softplus = jnp.maximum(pre, 0.0) + jnp.log(1.0 + jnp.exp(-jnp.abs(pre)))
    g_col = -jnp.exp(alog_ref[...]) * softplus
    beta_col = jax.nn.sigmoid(ab)

    row = lax.broadcasted_iota(jnp.int32, (c, c), 0)
    col = lax.broadcasted_iota(jnp.int32, (c, c), 1)
    lower = row >= col
    tri = lower.astype(BF16)

    mats, rhss, egs, qhs, qks, kts = [], [], [], [], [], []
    for ci in range(nchunk):
        r0 = ci * c
        gc_col = _tri_cumsum(tri, g_col[r0:r0 + c, :])
        gc_row = gc_col.T
        egs.append(jnp.exp(jnp.broadcast_to(gc_row[0:2 * SUBLANES, c - 1:c], (2 * SUBLANES, LANES))))
        for h in range(GDN_HEADS):
            q = y[r0:r0 + c, h * HEAD_DIM:(h + 1) * HEAD_DIM]
            k = y[r0:r0 + c, GDN_WIDTH + h * HEAD_DIM:GDN_WIDTH + (h + 1) * HEAD_DIM]
            v = y[r0:r0 + c, 2 * GDN_WIDTH + h * HEAD_DIM:2 * GDN_WIDTH + (h + 1) * HEAD_DIM]
            q = q * lax.rsqrt(jnp.sum(q * q, axis=-1, keepdims=True) + EPS) * (HEAD_DIM ** -0.5)
            k = k * lax.rsqrt(jnp.sum(k * k, axis=-1, keepdims=True) + EPS)
            beta = beta_col[r0:r0 + c, GDN_HEADS + h:GDN_HEADS + h + 1]
            gcl = gc_col[:, h:h + 1]
            gr = gc_row[h:h + 1, :]
            glast = gr[:, c - 1:c]
            decay = jnp.where(lower, jnp.exp(jnp.where(lower, gcl - gr, 0.0)), 0.0)
            kb = k * beta
            kbf = k.astype(BF16)
            mats.append(jnp.where(row > col, _dot_nt(kb.astype(BF16), kbf) * decay, 0.0))
            qk = _dot_nt(q.astype(BF16), kbf) * decay
            egc = jnp.exp(gcl)
            rhss.append(jnp.concatenate([v * beta, kb * egc], axis=1).astype(BF16))
            qhs.append((q * egc).astype(BF16))
            qks.append(qk.astype(BF16))
            kts.append((k * jnp.exp(glast - gcl)).T.astype(BF16))

    gain = on_ref[...]

    def recurrence_stages(ci, sols):
        rows = slice(ci * c, (ci + 1) * c)
        eg = egs[ci]
        probs = range(ci * GDN_HEADS, (ci + 1) * GDN_HEADS)
        live = {}

        def project():
            live["states"] = [s_ref[h] for h in range(GDN_HEADS)]
            live["sbs"] = [s.astype(BF16) for s in live["states"]]
            live["ws"] = [_dot(sol[:, HEAD_DIM:].astype(BF16), sb) for sol, sb in zip(sols, live["sbs"])]

        def residual():
            live["vbs"] = [(sol[:, :HEAD_DIM] - w).astype(BF16) for sol, w in zip(sols, live["ws"])]

        def advance():
            for h, (p, state, vb) in enumerate(zip(probs, live["states"], live["vbs"])):
                s_ref[h] = state * eg[h:h + 1, :] + _dot(kts[p], vb)

        def emit():
            outs = [_dot(qhs[p], sb) + _dot(qks[p], vb) for p, sb, vb in zip(probs, live["sbs"], live["vbs"])]
            for h, out in enumerate(outs):
                cs = slice(h * HEAD_DIM, (h + 1) * HEAD_DIM)
                zz = z_ref[rows, cs]
                o_ref[rows, cs] = (_rms(out, gain) * (zz * jax.nn.sigmoid(zz))).astype(o_ref.dtype)

        return (project, residual, advance), emit

    sols = [_dot(t.astype(BF16), rhs) for t, rhs in zip(_unit_lower_inverses(mats), rhss)]
    for ci in range(nchunk):
        chain, emit = recurrence_stages(ci, sols[ci * GDN_HEADS:(ci + 1) * GDN_HEADS])
        for stage in chain:
            stage()
        emit()


def gdn_core(qkv, ab, z, conv_w, alog_row, dtb_row, out_norm, seq, tl):
    t = qkv.shape[0]
    w3 = qkv.shape[1]
    width = w3 // 3
    nblk = t // tl
    prev_blocks = tl // SUBLANES
    tok = lambda i: (i, 0)
    const = lambda i: (0, 0)
    return pl.pallas_call(
        functools.partial(_gdn_core_kernel, blocks_per_seq=seq // tl),
        grid=(nblk,),
        in_specs=[pl.BlockSpec((tl, w3), tok),
                  pl.BlockSpec((SUBLANES, w3), lambda i: (jnp.maximum(i * prev_blocks - 1, 0), 0)),
                  pl.BlockSpec((tl, LANES), tok),
                  pl.BlockSpec((tl, width), tok),
                  pl.BlockSpec((CONV_WIDTH, w3), const),
                  pl.BlockSpec((1, LANES), const),
                  pl.BlockSpec((1, LANES), const),
                  pl.BlockSpec((1, HEAD_DIM), const)],
        out_specs=pl.BlockSpec((tl, width), tok),
        out_shape=jax.ShapeDtypeStruct((t, width), BF16),
        scratch_shapes=[pltpu.VMEM((GDN_HEADS, HEAD_DIM, HEAD_DIM), F32)],
        compiler_params=_params("arbitrary"),
        name="gdn_core",
    )(qkv, qkv, ab, z, conv_w, alog_row, dtb_row, out_norm.reshape(1, HEAD_DIM))


def _forget_cumsum_kernel(f_ref, b_ref, k_ref, cfm_ref, kp_ref, kn_ref, carry_ref):
    @pl.when(pl.program_id(1) == 0)
    def _():
        carry_ref[...] = jnp.zeros_like(carry_ref)

    n = f_ref.shape[0]
    pre = f_ref[...] + b_ref[...]
    logf = jnp.minimum(pre, 0.0) - jnp.log(1.0 + jnp.exp(-jnp.abs(pre)))
    row = lax.broadcasted_iota(jnp.int32, (n, n), 0)
    col = lax.broadcasted_iota(jnp.int32, (n, n), 1)
    cs = _tri_cumsum((row >= col).astype(BF16), logf) + carry_ref[...]
    carry_ref[...] = cs[n - 1:n, :]
    cfm_ref[0] = cs.T[:SUBLANES, :]
    hi, mid, lo = (p.astype(F32) for p in _split3(-LOG2E * cs))
    lane = lax.broadcasted_iota(jnp.int32, (n, LANES), 1)
    norms = []
    for h in range(FOX_HEADS):
        aug = jnp.where(lane < 3, 1.0,
                        jnp.where(lane == 3, hi[:, h:h + 1],
                                  jnp.where(lane == 4, mid[:, h:h + 1],
                                            jnp.where(lane == 5, lo[:, h:h + 1], 0.0))))
        kh = k_ref[:, h * HEAD_DIM:(h + 1) * HEAD_DIM]
        kp_ref[h, :, :HEAD_DIM] = kh
        kp_ref[h, :, HEAD_DIM:] = aug[:, :FOX_AUG].astype(BF16)
        kf = kh.astype(F32)
        kn2 = jnp.max(jnp.sum(kf * kf, axis=1, keepdims=True), axis=0, keepdims=True)
        norms.append(jnp.broadcast_to(kn2, (1, LANES)))
    kn_ref[0, 0] = jnp.concatenate(norms + [jnp.zeros((SUBLANES - FOX_HEADS, LANES), F32)], axis=0)


def forget_cumsum(flog, bias_row, k, batch, tl):
    t, width = k.shape
    seq = t // batch
    nblk = seq // tl
    tok = lambda b, i: (b * nblk + i, 0)
    return pl.pallas_call(
        _forget_cumsum_kernel,
        grid=(batch, nblk),
        in_specs=[pl.BlockSpec((tl, LANES), tok),
                  pl.BlockSpec((1, LANES), lambda b, i: (0, 0)),
                  pl.BlockSpec((tl, width), tok)],
        out_specs=[pl.BlockSpec((1, SUBLANES, tl), lambda b, i: (b, 0, i)),
                   pl.BlockSpec((FOX_HEADS, tl, HEAD_DIM + FOX_AUG), lambda b, i: (0, b * nblk + i, 0)),
                   pl.BlockSpec((1, 1, SUBLANES, LANES), lambda b, i: (b, i, 0, 0))],
        out_shape=[jax.ShapeDtypeStruct((batch, SUBLANES, seq), F32),
                   jax.ShapeDtypeStruct((FOX_HEADS, t, HEAD_DIM + FOX_AUG), BF16),
                   jax.ShapeDtypeStruct((batch, nblk, SUBLANES, LANES), F32)],
        scratch_shapes=[pltpu.VMEM((1, LANES), F32)],
        compiler_params=_params("parallel", "arbitrary"),
        name="forget_cumsum",
    )(flog, bias_row, k)


def _fox_in_kernel(x_ref, g_ref, w_ref, cfm_ref, kmax_ref, qp_ref, qm_ref, bnd_ref):
    xn = _rms(x_ref[...], g_ref[...]).astype(BF16)
    q = _dot(xn, w_ref[:, :FOX_WIDTH]) * (LOG2E * HEAD_DIM ** -0.5)
    qm_ref[...] = _dot(xn, w_ref[:, FOX_WIDTH:]).astype(qm_ref.dtype)
    qt = q.T
    tm = qt.shape[1]
    c2 = LOG2E * cfm_ref[0]
    hi, mid, lo = (p.astype(F32) for p in _split3(c2))
    kmax = kmax_ref[0, 0]
    r = lax.broadcasted_iota(jnp.int32, (FOX_AUG, tm), 0)
    bnds = []
    for h in range(FOX_HEADS):
        aug = jnp.where(r == 0, hi[h:h + 1, :],
                        jnp.where(r == 1, mid[h:h + 1, :],
                                  jnp.where(r == 2, lo[h:h + 1, :],
                                            jnp.where(r < 6, 1.0, 0.0))))
        qb = qt[h * HEAD_DIM:(h + 1) * HEAD_DIM, :].astype(BF16)
        qp_ref[h, :HEAD_DIM, :] = qb
        qp_ref[h, HEAD_DIM:, :] = aug.astype(BF16)
        qf = qb.astype(F32)
        q_norm = jnp.sqrt(jnp.sum(qf * qf, axis=0, keepdims=True))
        worst = jnp.max(2.0 * q_norm * kmax[h:h + 1, :1] + c2[h:h + 1, :], axis=1, keepdims=True)
        bnds.append(jnp.broadcast_to(worst + FOX_SKIP_LOG2, (1, LANES)))
    bnd_ref[0, 0] = jnp.concatenate(bnds + [jnp.zeros((SUBLANES - FOX_HEADS, LANES), F32)], axis=0)


def fox_in(x, gain, w, cfm, kmax, batch, tm):
    t, d = x.shape
    per_batch = t // batch // tm
    assert kmax.shape[1] == per_batch
    blk = lambda i: (i // per_batch, i % per_batch, 0, 0)
    return pl.pallas_call(
        _fox_in_kernel,
        grid=(t // tm,),
        in_specs=[pl.BlockSpec((tm, d), lambda i: (i, 0)),
                  pl.BlockSpec((1, d), lambda i: (0, 0)),
                  pl.BlockSpec(w.shape, lambda i: (0, 0)),
                  pl.BlockSpec((1, SUBLANES, tm), lambda i: (i // per_batch, 0, i % per_batch)),
                  pl.BlockSpec((1, 1, SUBLANES, LANES), blk)],
        out_specs=[pl.BlockSpec((FOX_HEADS, HEAD_DIM + FOX_AUG, tm), lambda i: (0, 0, i)),
                   pl.BlockSpec((tm, MEM_WIDTH), lambda i: (i, 0)),
                   pl.BlockSpec((1, 1, SUBLANES, LANES), blk)],
        out_shape=[jax.ShapeDtypeStruct((FOX_HEADS, HEAD_DIM + FOX_AUG, t), BF16),
                   jax.ShapeDtypeStruct((t, MEM_WIDTH), BF16),
                   jax.ShapeDtypeStruct((batch, per_batch, SUBLANES, LANES), F32)],
        compiler_params=_params("parallel"),
        name="fox_in",
    )(x, gain.reshape(1, d), w, cfm, kmax)


def _fox_kernel(it_ref, jt_ref, kind_ref, first_ref, qp_ref, kp_ref, vt_ref, o_ref, m_ref, acc_ref,
                *, tq, tk, nsteps):
    del it_ref, jt_ref
    step = pl.program_id(0) * nsteps + pl.program_id(1)
    kind = kind_ref[step]

    @pl.when(first_ref[step] == 1)
    def _():
        m_ref[...] = jnp.full_like(m_ref, NEG_INF)
        acc_ref[...] = jnp.zeros_like(acc_ref)

    ones_rows = (lax.broadcasted_iota(jnp.int32, (FOX_AUG, tk), 0) == 0).astype(BF16)

    def heads(masked):
        if masked:
            causal = (lax.broadcasted_iota(jnp.int32, (tk, tq), 0)
                      <= lax.broadcasted_iota(jnp.int32, (tk, tq), 1))
        ahead = 2
        scores = [_dot(kp_ref[h], qp_ref[h]) for h in range(ahead)]
        for h in range(FOX_HEADS):
            s = scores[h]
            if h + ahead < FOX_HEADS:
                scores.append(_dot(kp_ref[h + ahead], qp_ref[h + ahead]))
            if masked:
                s = jnp.where(causal, s, NEG_INF)
            m_prev = m_ref[h:h + 1, :]
            m_next = jnp.maximum(m_prev, jnp.max(s, axis=0, keepdims=True))
            alpha = jnp.exp2(m_prev - m_next)
            p = jnp.exp2(s - m_next).astype(BF16)
            vt = jnp.concatenate([vt_ref[h * HEAD_DIM:(h + 1) * HEAD_DIM, :], ones_rows], axis=0)
            acc_ref[h] = alpha * acc_ref[h] + _dot(vt, p)
            m_ref[h:h + 1, :] = m_next

    @pl.when(kind == 1)
    def _():
        heads(False)

    @pl.when(kind == 2)
    def _():
        heads(True)
        for h in range(FOX_HEADS):
            acc = acc_ref[h]
            out = acc[:HEAD_DIM, :] / acc[HEAD_DIM:HEAD_DIM + 1, :]
            o_ref[:, h * HEAD_DIM:(h + 1) * HEAD_DIM] = out.T.astype(o_ref.dtype)


def _prefix_max_key_norm(kn2):
    nblk = kn2.shape[1]
    blk = jnp.arange(nblk)
    upto = (blk[None, :] <= blk[:, None])[None, :, :, None, None]
    return jnp.sqrt(jnp.max(jnp.where(upto, kn2[:, None], 0.0), axis=2))


def _fox_first_needed(bnd, cfm, nq, tq):
    blk = jnp.arange(nq, dtype=jnp.int32)
    bnd = bnd[:, :, :FOX_HEADS, 0]
    cend = (LOG2E * cfm[:, :FOX_HEADS, tq - 1::tq]).transpose(0, 2, 1)
    negligible = jnp.all(cend[:, None, :, :] >= bnd[:, :, None, :], axis=-1)
    needed = (~negligible | (blk[None, :] >= blk[:, None])[None]).astype(jnp.int32)
    return jnp.sum((jnp.cumsum(needed, axis=2) == 0).astype(jnp.int32), axis=2)


def _fox_window_tables(j_first, nq, window):
    blk = jnp.arange(nq, dtype=jnp.int32)[None, :, None]
    want = j_first[:, :, None] + jnp.arange(window, dtype=jnp.int32)[None, None, :]
    valid = (want <= blk).astype(jnp.int32)
    it = jnp.broadcast_to(blk, want.shape)
    jt = jnp.minimum(want, blk)
    kind = valid * (1 + (jt == it).astype(jnp.int32))
    first = jnp.broadcast_to((jnp.arange(window) == 0).astype(jnp.int32)[None, None, :], want.shape)
    return it.reshape(-1), jt.reshape(-1), kind.reshape(-1), first.reshape(-1), nq * window


def _fox_compact_tables(j_first, nq):
    nsteps = nq * (nq + 1) // 2
    blk = jnp.arange(nq, dtype=jnp.int32)
    count = blk[None, :] + 1 - j_first
    end = jnp.cumsum(count, axis=1)
    start = end - count
    slot = jnp.arange(nsteps, dtype=jnp.int32)[None, :, None]
    it = jnp.minimum(jnp.sum((slot >= end[:, None, :]).astype(jnp.int32), axis=2), nq - 1)
    mine = (it[:, :, None] == blk[None, None, :]).astype(jnp.int32)
    jt = jnp.minimum(jnp.sum(mine * (j_first - start)[:, None, :], axis=2) + slot[:, :, 0], it)
    valid = (slot[:, :, 0] < end[:, nq - 1:nq]).astype(jnp.int32)
    kind = valid * (1 + (jt == it).astype(jnp.int32))
    first = valid * jnp.sum(mine * (start[:, None, :] == slot).astype(jnp.int32), axis=2)
    return it.reshape(-1), jt.reshape(-1), kind.reshape(-1), first.reshape(-1), nsteps


def fox_attention(qp, kp, vt, bnd, cfm, batch, tq):
    heads, kdim, t = qp.shape
    width = vt.shape[0]
    seq = t // batch
    nq = seq // tq
    tk = tq
    j_first = _fox_first_needed(bnd, cfm, nq, tq)

    def run(tables):
        i_tab, j_tab, kind, first, nsteps = tables
        qblk = lambda b, s, it, jt, kd, fs: b * nq + it[b * nsteps + s]
        kblk = lambda b, s, it, jt, kd, fs: b * nq + jt[b * nsteps + s]
        grid_spec = pltpu.PrefetchScalarGridSpec(
            num_scalar_prefetch=4,
            grid=(batch, nsteps),
            in_specs=[pl.BlockSpec((heads, kdim, tq), lambda *a: (0, 0, qblk(*a))),
                      pl.BlockSpec((heads, tk, kdim), lambda *a: (0, kblk(*a), 0)),
                      pl.BlockSpec((width, tk), lambda *a: (0, kblk(*a)))],
            out_specs=pl.BlockSpec((tq, width), lambda *a: (qblk(*a), 0)),
            scratch_shapes=[pltpu.VMEM((SUBLANES, tq), F32),
                            pltpu.VMEM((heads, kdim, tq), F32)])
        return pl.pallas_call(
            functools.partial(_fox_kernel, tq=tq, tk=tk, nsteps=nsteps),
            grid_spec=grid_spec,
            out_shape=jax.ShapeDtypeStruct((t, width), BF16),
            compiler_params=_params("parallel", "arbitrary"),
            name="fox_attention",
        )(i_tab, j_tab, kind, first, qp, kp, vt)

    window = min(FOX_WINDOW, nq)
    fits = jnp.all(jnp.arange(nq, dtype=jnp.int32)[None, :] + 1 - j_first <= window)
    return lax.cond(fits,
                    lambda: run(_fox_window_tables(j_first, nq, window)),
                    lambda: run(_fox_compact_tables(j_first, nq)))


def _mix_out_kernel(h_ref, main_ref, qm_ref, kbd_ref, vbd_ref, wo_ref, o_ref, *, main_width, mem_len):
    s = _dot(qm_ref[...], kbd_ref[0]) * (MEM_HEAD_DIM ** -0.5)
    acc = h_ref[...] + _dot(main_ref[...], wo_ref[:main_width, :])
    probs = []
    for g in range(MEM_HEADS):
        sg = s[:, g * mem_len:(g + 1) * mem_len]
        p = jnp.exp(sg - jnp.max(sg, axis=1, keepdims=True))
        probs.append((p / jnp.sum(p, axis=1, keepdims=True)).astype(BF16))
    mem_out = _dot(jnp.concatenate(probs, axis=1), vbd_ref[0])
    o_ref[...] = acc + _dot(mem_out.astype(BF16), wo_ref[main_width:, :])


def mix_out(h, main, qmem, kbd, vbd, wo, layer, batch, tm):
    t, d = h.shape
    main_width = main.shape[1]
    mem_len = kbd.shape[2] // MEM_HEADS
    per_batch = t // batch // tm
    tok = lambda i: (i, 0)
    return pl.pallas_call(
        functools.partial(_mix_out_kernel, main_width=main_width, mem_len=mem_len),
        grid=(t // tm,),
        in_specs=[pl.BlockSpec((tm, d), tok),
                  pl.BlockSpec((tm, main_width), tok),
                  pl.BlockSpec((tm, MEM_WIDTH), tok),
                  pl.BlockSpec((1,) + kbd.shape[1:], lambda i: (i // per_batch, 0, 0)),
                  pl.BlockSpec((1,) + vbd.shape[1:], lambda i: (i // per_batch, 0, 0)),
                  pl.BlockSpec((None,) + wo.shape[1:], lambda i: (layer, 0, 0))],
        out_specs=pl.BlockSpec((tm, d), tok),
        out_shape=jax.ShapeDtypeStruct((t, d), F32),
        compiler_params=_params("parallel"),
        name="mix_out",
    )(h, main, qmem, kbd, vbd, wo)


def _pad_cols(w, n):
    return jnp.pad(w, ((0, 0), (0, n - w.shape[1])))


def _lane_row(vec, offset=0):
    return jnp.zeros((1, LANES), F32).at[0, offset:offset + vec.shape[0]].set(vec.astype(F32))


def _memory_block_diag(kv, batch, mem_len):
    kv = kv.reshape(batch, mem_len, 2 * MEM_WIDTH)
    k, v = kv[..., :MEM_WIDTH], kv[..., MEM_WIDTH:]
    head_of_feat = jnp.arange(MEM_WIDTH) // MEM_HEAD_DIM
    head_of_slot = jnp.arange(MEM_HEADS * mem_len) // mem_len
    mask = head_of_feat[:, None] == head_of_slot[None, :]
    kbd = jnp.where(mask[None], jnp.tile(k.transpose(0, 2, 1), (1, 1, MEM_HEADS)), 0).astype(BF16)
    vbd = jnp.where(mask.T[None], jnp.tile(v, (1, MEM_HEADS, 1)), 0).astype(BF16)
    return kbd, vbd


def kernel(x, mem, ffn1_norm, ffn1_w_gate_up, ffn1_w_down, mix_norm, ffn2_norm, ffn2_w_gate_up,
           ffn2_w_down, gdn_w_in, gdn_conv, gdn_A_log, gdn_dt_bias, gdn_out_norm, fox_w_in, w_out,
           mem_norm, mem_w_kv, kv_norm, kv_w, kv_b_f, final_norm):
    bsz, seq, d = x.shape
    mem_len = mem.shape[1]
    depth = ffn1_norm.shape[0]
    n_a = gdn_w_in.shape[0]
    t = bsz * seq
    tm = min(512, seq)
    tf = 256
    tm_ffn = min(1024, seq)
    tl = min(256, seq)
    tq = min(512, seq)

    h = x.reshape(t, d).astype(F32)

    w_kv_all = jnp.concatenate([mem_w_kv[l] for l in range(depth)], axis=1).astype(BF16)
    (kv_mem,) = norm_matmul(mem.reshape(bsz * mem_len, d).astype(F32), mem_norm, w_kv_all,
                            [(0, w_kv_all.shape[1], False)], [F32], min(256, bsz * mem_len), "mem_kv")

    o0, o1 = 3 * GDN_WIDTH, 4 * GDN_WIDTH
    kp = vt = cfm = None
    wgu1, wd1 = ffn1_w_gate_up.astype(BF16), ffn1_w_down.astype(BF16)
    wgu2, wd2 = ffn2_w_gate_up.astype(BF16), ffn2_w_down.astype(BF16)
    wo_all = w_out.astype(BF16)
    for l in range(depth):
        last = l == depth - 1
        h = ffn(h, ffn1_norm[l], wgu1, wd1, l, final_norm, tm_ffn, tf, False, "ffn1")
        kbd, vbd = _memory_block_diag(kv_mem[:, 2 * MEM_WIDTH * l:2 * MEM_WIDTH * (l + 1)], bsz, mem_len)
        if l < n_a:
            qkv, z, ab, qmem = gdn_in(h, mix_norm[l], gdn_w_in.astype(F32), l, tm)
            alog_row = _lane_row(gdn_A_log[l])
            dtb_row = _lane_row(gdn_dt_bias[l])
            main = gdn_core(qkv, ab, z, gdn_conv[l].astype(F32), alog_row, dtb_row,
                            gdn_out_norm[l].astype(F32), seq, tl)
        else:
            qp, qmem, bnd = fox_in(h, mix_norm[l], fox_w_in[l - n_a].astype(BF16), cfm, kmax, bsz, tq)
            main = fox_attention(qp, kp, vt, bnd, cfm, bsz, tq)
        h = mix_out(h, main, qmem, kbd, vbd, wo_all, l, bsz, tm)
        h = ffn(h, ffn2_norm[l], wgu2, wd2, l, final_norm, tm_ffn, tf, last, "ffn2")
        if l == n_a - 1:
            w_cat = jnp.concatenate([kv_w[:, :2 * FOX_WIDTH], _pad_cols(kv_w[:, 2 * FOX_WIDTH:], LANES)],
                                    axis=1).astype(BF16)
            vt, k_sh, flog = norm_matmul(
                h, kv_norm, w_cat,
                [(FOX_WIDTH, 2 * FOX_WIDTH, True), (0, FOX_WIDTH, False),
                 (2 * FOX_WIDTH, 2 * FOX_WIDTH + LANES, False)],
                [BF16, BF16, F32], tm, "shared_kv")
            cfm, kp, kn2 = forget_cumsum(flog, _lane_row(kv_b_f), k_sh, bsz, tq)
            kmax = _prefix_max_key_norm(kn2)
    return h.reshape(bsz, seq, d)
```

```python
import functools

import jax
import jax.numpy as jnp
from jax import lax
from jax.experimental import pallas as pl
from jax.experimental.pallas import tpu as pltpu

F32 = jnp.float32
BF16 = jnp.bfloat16

EPS = 1e-6
NEG_INF = -1e30
HEAD_DIM = 128
GDN_HEADS = 6
GDN_WIDTH = GDN_HEADS * HEAD_DIM
FOX_HEADS = 6
FOX_WIDTH = FOX_HEADS * HEAD_DIM
CONV_WIDTH = 4
MEM_HEADS = 4
MEM_HEAD_DIM = 64
MEM_WIDTH = MEM_HEADS * MEM_HEAD_DIM

LANES = 128
SUBLANES = 8
GDN_CHUNK = 128
FOX_AUG = 16
LOG2E = 1.4426950408889634
FOX_SKIP_LOG2 = 160.0
FOX_WINDOW = 4
VMEM_LIMIT = 56 * 1024 * 1024


def _params(*sem):
    return pltpu.CompilerParams(dimension_semantics=sem, vmem_limit_bytes=VMEM_LIMIT)


def _dot(a, b):
    return jnp.dot(a, b, preferred_element_type=F32)


def _dot_nt(a, b):
    return lax.dot_general(a, b, (((1,), (1,)), ((), ())), preferred_element_type=F32)


def _rms(x, gain):
    return x * lax.rsqrt(jnp.mean(x * x, axis=-1, keepdims=True) + EPS) * gain


def _split3(x):
    hi = x.astype(BF16)
    r1 = x - hi.astype(F32)
    mid = r1.astype(BF16)
    lo = (r1 - mid.astype(F32)).astype(BF16)
    return hi, mid, lo


def _tri_cumsum(tri, x):
    hi, mid, lo = _split3(x)
    return _dot(tri, hi) + _dot(tri, mid) + _dot(tri, lo)


def _norm_matmul_kernel(x_ref, g_ref, w_ref, *out_refs, splits):
    xn = _rms(x_ref[...], g_ref[...]).astype(BF16)
    ys = [_dot(xn, w_ref[:, lo:hi]) for lo, hi, _ in splits]
    for (lo, hi, transposed), o_ref, y in zip(splits, out_refs, ys):
        if transposed:
            o_ref[...] = y.T.astype(o_ref.dtype)
        else:
            o_ref[...] = y.astype(o_ref.dtype)


def norm_matmul(x, gain, w, splits, dtypes, tm, name):
    t, d = x.shape
    n = w.shape[1]
    out_shape, out_specs = [], []
    for (lo, hi, transposed), dt in zip(splits, dtypes):
        if transposed:
            out_shape.append(jax.ShapeDtypeStruct((hi - lo, t), dt))
            out_specs.append(pl.BlockSpec((hi - lo, tm), lambda i: (0, i)))
        else:
            out_shape.append(jax.ShapeDtypeStruct((t, hi - lo), dt))
            out_specs.append(pl.BlockSpec((tm, hi - lo), lambda i: (i, 0)))
    return pl.pallas_call(
        functools.partial(_norm_matmul_kernel, splits=tuple(splits)),
        grid=(t // tm,),
        in_specs=[pl.BlockSpec((tm, d), lambda i: (i, 0)),
                  pl.BlockSpec((1, d), lambda i: (0, 0)),
                  pl.BlockSpec((d, n), lambda i: (0, 0))],
        out_specs=out_specs,
        out_shape=out_shape,
        compiler_params=_params("parallel"),
        name=name,
    )(x, gain.reshape(1, d), w)


def _ffn_kernel(h_ref, g_ref, wgu_ref, wd_ref, fg_ref, o_ref, act_ref, *, hidden, tf, final_norm):
    x = h_ref[...]
    xn = _rms(x, g_ref[...]).astype(BF16)
    for j in range(hidden // tf):
        gate = _dot(xn, wgu_ref[:, j * tf:(j + 1) * tf])
        up = _dot(xn, wgu_ref[:, hidden + j * tf:hidden + (j + 1) * tf])
        act_ref[:, j * tf:(j + 1) * tf] = (gate * jax.nn.sigmoid(gate) * up).astype(BF16)
    out = x + 0.5 * _dot(act_ref[...], wd_ref[...])
    if final_norm:
        out = _rms(out, fg_ref[...])
    o_ref[...] = out


def ffn(h, gain, wgu, wd, layer, final_gain, tm, tf, final_norm, name):
    t, d = h.shape
    hidden = wd.shape[1]
    return pl.pallas_call(
        functools.partial(_ffn_kernel, hidden=hidden, tf=tf, final_norm=final_norm),
        grid=(t // tm,),
        in_specs=[pl.BlockSpec((tm, d), lambda i: (i, 0)),
                  pl.BlockSpec((1, d), lambda i: (0, 0)),
                  pl.BlockSpec((None, d, 2 * hidden), lambda i: (layer, 0, 0), pipeline_mode=pl.Buffered(1)),
                  pl.BlockSpec((None, hidden, d), lambda i: (layer, 0, 0), pipeline_mode=pl.Buffered(1)),
                  pl.BlockSpec((1, d), lambda i: (0, 0))],
        out_specs=pl.BlockSpec((tm, d), lambda i: (i, 0)),
        out_shape=jax.ShapeDtypeStruct((t, d), F32),
        scratch_shapes=[pltpu.VMEM((tm, hidden), BF16)],
        compiler_params=_params("parallel"),
        name=name,
    )(h, gain.reshape(1, d), wgu, wd, final_gain.reshape(1, d))


def _unit_lower_inverses(mats):
    n = mats[0].shape[0]
    row = lax.broadcasted_iota(jnp.int32, (n, n), 0)
    col = lax.broadcasted_iota(jnp.int32, (n, n), 1)
    eye = (row == col).astype(F32)
    pair = (row >> 1) == (col >> 1)
    ts = [eye - jnp.where(pair, a, 0.0) for a in mats]
    s = 2
    while s < n:
        k = s.bit_length() - 1
        sub = ((row >> (k + 1)) == (col >> (k + 1))) & ((row >> k) != (col >> k))
        tbs = [t.astype(BF16) for t in ts]
        tes = [_dot(tb, jnp.where(sub, a, 0.0).astype(BF16)).astype(BF16) for tb, a in zip(tbs, mats)]
        ts = [t - _dot(te, tb) for t, te, tb in zip(ts, tes, tbs)]
        s *= 2
    return ts


def _conv_silu(x, prev, cw):
    ln = x.shape[0]
    xs = jnp.concatenate([prev, x], axis=0)
    y = cw[CONV_WIDTH - 1:CONV_WIDTH, :] * x
    for tap in range(CONV_WIDTH - 1):
        shift = CONV_WIDTH - 1 - tap
        y = y + cw[tap:tap + 1, :] * xs[SUBLANES - shift:SUBLANES - shift + ln, :]
    return y * jax.nn.sigmoid(y)


def _gdn_in_kernel(x_ref, g_ref, w_ref, qkv_ref, z_ref, ab_ref, qm_ref, wb_ref):
    @pl.when(pl.program_id(0) == 0)
    def _():
        wb_ref[...] = w_ref[...].astype(BF16)

    o0, o1 = 3 * GDN_WIDTH, 4 * GDN_WIDTH
    ab_off = 2 * GDN_HEADS
    xn = _rms(x_ref[...], g_ref[...]).astype(BF16)
    qkv_ref[...] = _dot(xn, wb_ref[:, :o0])
    z_ref[...] = _dot(xn, wb_ref[:, o0:o1])
    rest = _dot(xn, wb_ref[:, o1:])
    ab_ref[...] = rest[:, :LANES]
    qm_ref[...] = rest[:, ab_off:ab_off + MEM_WIDTH].astype(qm_ref.dtype)


def gdn_in(h, gain, w_in, layer, tm):
    t, d = h.shape
    n = w_in.shape[2]
    w3 = 3 * GDN_WIDTH
    tok = lambda i: (i, 0)
    const = lambda i: (0, 0)
    return pl.pallas_call(
        _gdn_in_kernel,
        grid=(t // tm,),
        in_specs=[pl.BlockSpec((tm, d), tok),
                  pl.BlockSpec((1, d), const),
                  pl.BlockSpec((None, d, n), lambda i: (layer, 0, 0), pipeline_mode=pl.Buffered(1))],
        out_specs=[pl.BlockSpec((tm, w3), tok),
                   pl.BlockSpec((tm, GDN_WIDTH), tok),
                   pl.BlockSpec((tm, LANES), tok),
                   pl.BlockSpec((tm, MEM_WIDTH), tok)],
        out_shape=[jax.ShapeDtypeStruct((t, w3), F32),
                   jax.ShapeDtypeStruct((t, GDN_WIDTH), F32),
                   jax.ShapeDtypeStruct((t, LANES), F32),
                   jax.ShapeDtypeStruct((t, MEM_WIDTH), BF16)],
        scratch_shapes=[pltpu.VMEM((d, n), BF16)],
        compiler_params=_params("arbitrary"),
        name="gdn_in",
    )(h, gain.reshape(1, d), w_in)


def _gdn_core_kernel(x_ref, xp_ref, ab_ref, z_ref, cw_ref, alog_ref, dtb_ref, on_ref, o_ref, s_ref,
                     *, blocks_per_seq):
    c = GDN_CHUNK
    nchunk = x_ref.shape[0] // c
    first = (pl.program_id(0) % blocks_per_seq) == 0

    @pl.when(first)
    def _():
        s_ref[...] = jnp.zeros_like(s_ref)

    prev = jnp.where(first, 0.0, xp_ref[...])
    y = _conv_silu(x_ref[...], prev, cw_ref[...])

    ab = ab_ref[...]
    pre = ab + dtb_ref[...]
    softplus = jnp.maximum(pre, 0.0) + jnp.log(1.0 + jnp.exp(-jnp.abs(pre)))
    g_col = -jnp.exp(alog_ref[...]) * softplus
    beta_col = jax.nn.sigmoid(ab)

    row = lax.broadcasted_iota(jnp.int32, (c, c), 0)
    col = lax.broadcasted_iota(jnp.int32, (c, c), 1)
    lower = row >= col
    tri = lower.astype(BF16)

    mats, rhss, egs, qhs, qks, kts = [], [], [], [], [], []
    for ci in range(nchunk):
        r0 = ci * c
        gc_col = _tri_cumsum(tri, g_col[r0:r0 + c, :])
        gc_row = gc_col.T
        egs.append(jnp.exp(jnp.broadcast_to(gc_row[0:2 * SUBLANES, c - 1:c], (2 * SUBLANES, LANES))))
        for h in range(GDN_HEADS):
            q = y[r0:r0 + c, h * HEAD_DIM:(h + 1) * HEAD_DIM]
            k = y[r0:r0 + c, GDN_WIDTH + h * HEAD_DIM:GDN_WIDTH + (h + 1) * HEAD_DIM]
            v = y[r0:r0 + c, 2 * GDN_WIDTH + h * HEAD_DIM:2 * GDN_WIDTH + (h + 1) * HEAD_DIM]
            q = q * lax.rsqrt(jnp.sum(q * q, axis=-1, keepdims=True) + EPS) * (HEAD_DIM ** -0.5)
            k = k * lax.rsqrt(jnp.sum(k * k, axis=-1, keepdims=True) + EPS)
            beta = beta_col[r0:r0 + c, GDN_HEADS + h:GDN_HEADS + h + 1]
            gcl = gc_col[:, h:h + 1]
            gr = gc_row[h:h + 1, :]
            glast = gr[:, c - 1:c]
            decay = jnp.where(lower, jnp.exp(jnp.where(lower, gcl - gr, 0.0)), 0.0)
            kb = k * beta
            kbf = k.astype(BF16)
            mats.append(jnp.where(row > col, _dot_nt(kb.astype(BF16), kbf) * decay, 0.0))
            qk = _dot_nt(q.astype(BF16), kbf) * decay
            egc = jnp.exp(gcl)
            rhss.append(jnp.concatenate([v * beta, kb * egc], axis=1).astype(BF16))
            qhs.append((q * egc).astype(BF16))
            qks.append(qk.astype(BF16))
            kts.append((k * jnp.exp(glast - gcl)).T.astype(BF16))

    gain = on_ref[...]

    def recurrence_stages(ci, sols):
        rows = slice(ci * c, (ci + 1) * c)
        eg = egs[ci]
        probs = range(ci * GDN_HEADS, (ci + 1) * GDN_HEADS)
        live = {}

        def project():
            live["states"] = [s_ref[h] for h in range(GDN_HEADS)]
            live["sbs"] = [s.astype(BF16) for s in live["states"]]
            live["ws"] = [_dot(sol[:, HEAD_DIM:].astype(BF16), sb) for sol, sb in zip(sols, live["sbs"])]

        def residual():
            live["vbs"] = [(sol[:, :HEAD_DIM] - w).astype(BF16) for sol, w in zip(sols, live["ws"])]

        def advance():
            for h, (p, state, vb) in enumerate(zip(probs, live["states"], live["vbs"])):
                s_ref[h] = state * eg[h:h + 1, :] + _dot(kts[p], vb)

        def emit():
            outs = [_dot(qhs[p], sb) + _dot(qks[p], vb) for p, sb, vb in zip(probs, live["sbs"], live["vbs"])]
            for h, out in enumerate(outs):
                cs = slice(h * HEAD_DIM, (h + 1) * HEAD_DIM)
                zz = z_ref[rows, cs]
                o_ref[rows, cs] = (_rms(out, gain) * (zz * jax.nn.sigmoid(zz))).astype(o_ref.dtype)

        return (project, residual, advance), emit

    sols = [_dot(t.astype(BF16), rhs) for t, rhs in zip(_unit_lower_inverses(mats), rhss)]
    for ci in range(nchunk):
        chain, emit = recurrence_stages(ci, sols[ci * GDN_HEADS:(ci + 1) * GDN_HEADS])
        for stage in chain:
            stage()
        emit()


def gdn_core(qkv, ab, z, conv_w, alog_row, dtb_row, out_norm, seq, tl):
    t = qkv.shape[0]
    w3 = qkv.shape[1]
    width = w3 // 3
    nblk = t // tl
    prev_blocks = tl // SUBLANES
    tok = lambda i: (i, 0)
    const = lambda i: (0, 0)
    return pl.pallas_call(
        functools.partial(_gdn_core_kernel, blocks_per_seq=seq // tl),
        grid=(nblk,),
        in_specs=[pl.BlockSpec((tl, w3), tok),
                  pl.BlockSpec((SUBLANES, w3), lambda i: (jnp.maximum(i * prev_blocks - 1, 0), 0)),
                  pl.BlockSpec((tl, LANES), tok),
                  pl.BlockSpec((tl, width), tok),
                  pl.BlockSpec((CONV_WIDTH, w3), const),
                  pl.BlockSpec((1, LANES), const),
                  pl.BlockSpec((1, LANES), const),
                  pl.BlockSpec((1, HEAD_DIM), const)],
        out_specs=pl.BlockSpec((tl, width), tok),
        out_shape=jax.ShapeDtypeStruct((t, width), BF16),
        scratch_shapes=[pltpu.VMEM((GDN_HEADS, HEAD_DIM, HEAD_DIM), F32)],
        compiler_params=_params("arbitrary"),
        name="gdn_core",
    )(qkv, qkv, ab, z, conv_w, alog_row, dtb_row, out_norm.reshape(1, HEAD_DIM))


def _forget_cumsum_kernel(f_ref, b_ref, k_ref, cfm_ref, kp_ref, kn_ref, carry_ref):
    @pl.when(pl.program_id(1) == 0)
    def _():
        carry_ref[...] = jnp.zeros_like(carry_ref)

    n = f_ref.shape[0]
    pre = f_ref[...] + b_ref[...]
    logf = jnp.minimum(pre, 0.0) - jnp.log(1.0 + jnp.exp(-jnp.abs(pre)))
    row = lax.broadcasted_iota(jnp.int32, (n, n), 0)
    col = lax.broadcasted_iota(jnp.int32, (n, n), 1)
    cs = _tri_cumsum((row >= col).astype(BF16), logf) + carry_ref[...]
    carry_ref[...] = cs[n - 1:n, :]
    cfm_ref[0] = cs.T[:SUBLANES, :]
    hi, mid, lo = (p.astype(F32) for p in _split3(-LOG2E * cs))
    lane = lax.broadcasted_iota(jnp.int32, (n, LANES), 1)
    norms = []
    for h in range(FOX_HEADS):
        aug = jnp.where(lane < 3, 1.0,
                        jnp.where(lane == 3, hi[:, h:h + 1],
                                  jnp.where(lane == 4, mid[:, h:h + 1],
                                            jnp.where(lane == 5, lo[:, h:h + 1], 0.0))))
        kh = k_ref[:, h * HEAD_DIM:(h + 1) * HEAD_DIM]
        kp_ref[h, :, :HEAD_DIM] = kh
        kp_ref[h, :, HEAD_DIM:] = aug[:, :FOX_AUG].astype(BF16)
        kf = kh.astype(F32)
        kn2 = jnp.max(jnp.sum(kf * kf, axis=1, keepdims=True), axis=0, keepdims=True)
        norms.append(jnp.broadcast_to(kn2, (1, LANES)))
    kn_ref[0, 0] = jnp.concatenate(norms + [jnp.zeros((SUBLANES - FOX_HEADS, LANES), F32)], axis=0)


def forget_cumsum(flog, bias_row, k, batch, tl):
    t, width = k.shape
    seq = t // batch
    nblk = seq // tl
    tok = lambda b, i: (b * nblk + i, 0)
    return pl.pallas_call(
        _forget_cumsum_kernel,
        grid=(batch, nblk),
        in_specs=[pl.BlockSpec((tl, LANES), tok),
                  pl.BlockSpec((1, LANES), lambda b, i: (0, 0)),
                  pl.BlockSpec((tl, width), tok)],
        out_specs=[pl.BlockSpec((1, SUBLANES, tl), lambda b, i: (b, 0, i)),
                   pl.BlockSpec((FOX_HEADS, tl, HEAD_DIM + FOX_AUG), lambda b, i: (0, b * nblk + i, 0)),
                   pl.BlockSpec((1, 1, SUBLANES, LANES), lambda b, i: (b, i, 0, 0))],
        out_shape=[jax.ShapeDtypeStruct((batch, SUBLANES, seq), F32),
                   jax.ShapeDtypeStruct((FOX_HEADS, t, HEAD_DIM + FOX_AUG), BF16),
                   jax.ShapeDtypeStruct((batch, nblk, SUBLANES, LANES), F32)],
        scratch_shapes=[pltpu.VMEM((1, LANES), F32)],
        compiler_params=_params("parallel", "arbitrary"),
        name="forget_cumsum",
    )(flog, bias_row, k)


def _fox_in_kernel(x_ref, g_ref, w_ref, cfm_ref, kmax_ref, qp_ref, qm_ref, bnd_ref):
    xn = _rms(x_ref[...], g_ref[...]).astype(BF16)
    q = _dot(xn, w_ref[:, :FOX_WIDTH]) * (LOG2E * HEAD_DIM ** -0.5)
    qm_ref[...] = _dot(xn, w_ref[:, FOX_WIDTH:]).astype(qm_ref.dtype)
    qt = q.T
    tm = qt.shape[1]
    c2 = LOG2E * cfm_ref[0]
    hi, mid, lo = (p.astype(F32) for p in _split3(c2))
    kmax = kmax_ref[0, 0]
    r = lax.broadcasted_iota(jnp.int32, (FOX_AUG, tm), 0)
    bnds = []
    for h in range(FOX_HEADS):
        aug = jnp.where(r == 0, hi[h:h + 1, :],
                        jnp.where(r == 1, mid[h:h + 1, :],
                                  jnp.where(r == 2, lo[h:h + 1, :],
                                            jnp.where(r < 6, 1.0, 0.0))))
        qb = qt[h * HEAD_DIM:(h + 1) * HEAD_DIM, :].astype(BF16)
        qp_ref[h, :HEAD_DIM, :] = qb
        qp_ref[h, HEAD_DIM:, :] = aug.astype(BF16)
        qf = qb.astype(F32)
        q_norm = jnp.sqrt(jnp.sum(qf * qf, axis=0, keepdims=True))
        worst = jnp.max(2.0 * q_norm * kmax[h:h + 1, :1] + c2[h:h + 1, :], axis=1, keepdims=True)
        bnds.append(jnp.broadcast_to(worst + FOX_SKIP_LOG2, (1, LANES)))
    bnd_ref[0, 0] = jnp.concatenate(bnds + [jnp.zeros((SUBLANES - FOX_HEADS, LANES), F32)], axis=0)


def fox_in(x, gain, w, cfm, kmax, batch, tm):
    t, d = x.shape
    per_batch = t // batch // tm
    assert kmax.shape[1] == per_batch
    blk = lambda i: (i // per_batch, i % per_batch, 0, 0)
    return pl.pallas_call(
        _fox_in_kernel,
        grid=(t // tm,),
        in_specs=[pl.BlockSpec((tm, d), lambda i: (i, 0)),
                  pl.BlockSpec((1, d), lambda i: (0, 0)),
                  pl.BlockSpec(w.shape, lambda i: (0, 0)),
                  pl.BlockSpec((1, SUBLANES, tm), lambda i: (i // per_batch, 0, i % per_batch)),
                  pl.BlockSpec((1, 1, SUBLANES, LANES), blk)],
        out_specs=[pl.BlockSpec((FOX_HEADS, HEAD_DIM + FOX_AUG, tm), lambda i: (0, 0, i)),
                   pl.BlockSpec((tm, MEM_WIDTH), lambda i: (i, 0)),
                   pl.BlockSpec((1, 1, SUBLANES, LANES), blk)],
        out_shape=[jax.ShapeDtypeStruct((FOX_HEADS, HEAD_DIM + FOX_AUG, t), BF16),
                   jax.ShapeDtypeStruct((t, MEM_WIDTH), BF16),
                   jax.ShapeDtypeStruct((batch, per_batch, SUBLANES, LANES), F32)],
        compiler_params=_params("parallel"),
        name="fox_in",
    )(x, gain.reshape(1, d), w, cfm, kmax)


def _fox_kernel(it_ref, jt_ref, kind_ref, first_ref, qp_ref, kp_ref, vt_ref, o_ref, m_ref, acc_ref,
                *, tq, tk, nsteps):
    del it_ref, jt_ref
    step = pl.program_id(0) * nsteps + pl.program_id(1)
    kind = kind_ref[step]

    @pl.when(first_ref[step] == 1)
    def _():
        m_ref[...] = jnp.full_like(m_ref, NEG_INF)
        acc_ref[...] = jnp.zeros_like(acc_ref)

    ones_rows = (lax.broadcasted_iota(jnp.int32, (FOX_AUG, tk), 0) == 0).astype(BF16)

    def heads(masked):
        if masked:
            causal = (lax.broadcasted_iota(jnp.int32, (tk, tq), 0)
                      <= lax.broadcasted_iota(jnp.int32, (tk, tq), 1))
        ahead = 2
        scores = [_dot(kp_ref[h], qp_ref[h]) for h in range(ahead)]
        for h in range(FOX_HEADS):
            s = scores[h]
            if h + ahead < FOX_HEADS:
                scores.append(_dot(kp_ref[h + ahead], qp_ref[h + ahead]))
            if masked:
                s = jnp.where(causal, s, NEG_INF)
            m_prev = m_ref[h:h + 1, :]
            m_next = jnp.maximum(m_prev, jnp.max(s, axis=0, keepdims=True))
            alpha = jnp.exp2(m_prev - m_next)
            p = jnp.exp2(s - m_next).astype(BF16)
            vt = jnp.concatenate([vt_ref[h * HEAD_DIM:(h + 1) * HEAD_DIM, :], ones_rows], axis=0)
            acc_ref[h] = alpha * acc_ref[h] + _dot(vt, p)
            m_ref[h:h + 1, :] = m_next

    @pl.when(kind == 1)
    def _():
        heads(False)

    @pl.when(kind == 2)
    def _():
        heads(True)
        for h in range(FOX_HEADS):
            acc = acc_ref[h]
            out = acc[:HEAD_DIM, :] / acc[HEAD_DIM:HEAD_DIM + 1, :]
            o_ref[:, h * HEAD_DIM:(h + 1) * HEAD_DIM] = out.T.astype(o_ref.dtype)


def _prefix_max_key_norm(kn2):
    nblk = kn2.shape[1]
    blk = jnp.arange(nblk)
    upto = (blk[None, :] <= blk[:, None])[None, :, :, None, None]
    return jnp.sqrt(jnp.max(jnp.where(upto, kn2[:, None], 0.0), axis=2))


def _fox_first_needed(bnd, cfm, nq, tq):
    blk = jnp.arange(nq, dtype=jnp.int32)
    bnd = bnd[:, :, :FOX_HEADS, 0]
    cend = (LOG2E * cfm[:, :FOX_HEADS, tq - 1::tq]).transpose(0, 2, 1)
    negligible = jnp.all(cend[:, None, :, :] >= bnd[:, :, None, :], axis=-1)
    needed = (~negligible | (blk[None, :] >= blk[:, None])[None]).astype(jnp.int32)
    return jnp.sum((jnp.cumsum(needed, axis=2) == 0).astype(jnp.int32), axis=2)


def _fox_compact_tables(j_first, nq, nsteps):
    blk = jnp.arange(nq, dtype=jnp.int32)
    count = blk[None, :] + 1 - j_first
    end = jnp.cumsum(count, axis=1)
    start = end - count
    slot = jnp.arange(nsteps, dtype=jnp.int32)[None, :, None]
    it = jnp.minimum(jnp.sum((slot >= end[:, None, :]).astype(jnp.int32), axis=2), nq - 1)
    mine = (it[:, :, None] == blk[None, None, :]).astype(jnp.int32)
    jt = jnp.minimum(jnp.sum(mine * (j_first - start)[:, None, :], axis=2) + slot[:, :, 0], it)
    valid = (slot[:, :, 0] < end[:, nq - 1:nq]).astype(jnp.int32)
    kind = valid * (1 + (jt == it).astype(jnp.int32))
    first = valid * jnp.sum(mine * (start[:, None, :] == slot).astype(jnp.int32), axis=2)
    return it.reshape(-1), jt.reshape(-1), kind.reshape(-1), first.reshape(-1), nsteps


def fox_attention(qp, kp, vt, bnd, cfm, batch, tq):
    heads, kdim, t = qp.shape
    width = vt.shape[0]
    seq = t // batch
    nq = seq // tq
    tk = tq
    j_first = _fox_first_needed(bnd, cfm, nq, tq)

    def run(tables):
        i_tab, j_tab, kind, first, nsteps = tables
        qblk = lambda b, s, it, jt, kd, fs: b * nq + it[b * nsteps + s]
        kblk = lambda b, s, it, jt, kd, fs: b * nq + jt[b * nsteps + s]
        grid_spec = pltpu.PrefetchScalarGridSpec(
            num_scalar_prefetch=4,
            grid=(batch, nsteps),
            in_specs=[pl.BlockSpec((heads, kdim, tq), lambda *a: (0, 0, qblk(*a))),
                      pl.BlockSpec((heads, tk, kdim), lambda *a: (0, kblk(*a), 0)),
                      pl.BlockSpec((width, tk), lambda *a: (0, kblk(*a)))],
            out_specs=pl.BlockSpec((tq, width), lambda *a: (qblk(*a), 0)),
            scratch_shapes=[pltpu.VMEM((SUBLANES, tq), F32),
                            pltpu.VMEM((heads, kdim, tq), F32)])
        return pl.pallas_call(
            functools.partial(_fox_kernel, tq=tq, tk=tk, nsteps=nsteps),
            grid_spec=grid_spec,
            out_shape=jax.ShapeDtypeStruct((t, width), BF16),
            compiler_params=_params("parallel", "arbitrary"),
            name="fox_attention",
        )(i_tab, j_tab, kind, first, qp, kp, vt)

    all_pairs = nq * (nq + 1) // 2
    short = min(FOX_WINDOW * nq, all_pairs)
    needed = jnp.max(jnp.sum(jnp.arange(nq, dtype=jnp.int32)[None, :] + 1 - j_first, axis=1))
    return lax.cond(needed <= short,
                    lambda: run(_fox_compact_tables(j_first, nq, short)),
                    lambda: run(_fox_compact_tables(j_first, nq, all_pairs)))


def _mix_out_kernel(h_ref, main_ref, qm_ref, kbd_ref, vbd_ref, wo_ref, o_ref, *, main_width, mem_len):
    s = _dot(qm_ref[...], kbd_ref[0]) * (MEM_HEAD_DIM ** -0.5)
    acc = h_ref[...] + _dot(main_ref[...], wo_ref[:main_width, :])
    probs = []
    for g in range(MEM_HEADS):
        sg = s[:, g * mem_len:(g + 1) * mem_len]
        p = jnp.exp(sg - jnp.max(sg, axis=1, keepdims=True))
        probs.append((p / jnp.sum(p, axis=1, keepdims=True)).astype(BF16))
    mem_out = _dot(jnp.concatenate(probs, axis=1), vbd_ref[0])
    o_ref[...] = acc + _dot(mem_out.astype(BF16), wo_ref[main_width:, :])


def mix_out(h, main, qmem, kbd, vbd, wo, layer, batch, tm):
    t, d = h.shape
    main_width = main.shape[1]
    mem_len = kbd.shape[2] // MEM_HEADS
    per_batch = t // batch // tm
    tok = lambda i: (i, 0)
    return pl.pallas_call(
        functools.partial(_mix_out_kernel, main_width=main_width, mem_len=mem_len),
        grid=(t // tm,),
        in_specs=[pl.BlockSpec((tm, d), tok),
                  pl.BlockSpec((tm, main_width), tok),
                  pl.BlockSpec((tm, MEM_WIDTH), tok),
                  pl.BlockSpec((1,) + kbd.shape[1:], lambda i: (i // per_batch, 0, 0)),
                  pl.BlockSpec((1,) + vbd.shape[1:], lambda i: (i // per_batch, 0, 0)),
                  pl.BlockSpec((None,) + wo.shape[1:], lambda i: (layer, 0, 0))],
        out_specs=pl.BlockSpec((tm, d), tok),
        out_shape=jax.ShapeDtypeStruct((t, d), F32),
        compiler_params=_params("parallel"),
        name="mix_out",
    )(h, main, qmem, kbd, vbd, wo)


def _pad_cols(w, n):
    return jnp.pad(w, ((0, 0), (0, n - w.shape[1])))


def _lane_row(vec, offset=0):
    return jnp.zeros((1, LANES), F32).at[0, offset:offset + vec.shape[0]].set(vec.astype(F32))


def _memory_block_diag(kv, batch, mem_len):
    kv = kv.reshape(batch, mem_len, 2 * MEM_WIDTH)
    k, v = kv[..., :MEM_WIDTH], kv[..., MEM_WIDTH:]
    head_of_feat = jnp.arange(MEM_WIDTH) // MEM_HEAD_DIM
    head_of_slot = jnp.arange(MEM_HEADS * mem_len) // mem_len
    mask = head_of_feat[:, None] == head_of_slot[None, :]
    kbd = jnp.where(mask[None], jnp.tile(k.transpose(0, 2, 1), (1, 1, MEM_HEADS)), 0).astype(BF16)
    vbd = jnp.where(mask.T[None], jnp.tile(v, (1, MEM_HEADS, 1)), 0).astype(BF16)
    return kbd, vbd


def kernel(x, mem, ffn1_norm, ffn1_w_gate_up, ffn1_w_down, mix_norm, ffn2_norm, ffn2_w_gate_up,
           ffn2_w_down, gdn_w_in, gdn_conv, gdn_A_log, gdn_dt_bias, gdn_out_norm, fox_w_in, w_out,
           mem_norm, mem_w_kv, kv_norm, kv_w, kv_b_f, final_norm):
    bsz, seq, d = x.shape
    mem_len = mem.shape[1]
    depth = ffn1_norm.shape[0]
    n_a = gdn_w_in.shape[0]
    t = bsz * seq
    tm = min(512, seq)
    tf = 256
    tm_ffn = min(1024, seq)
    tl = min(256, seq)
    tq = min(512, seq)

    h = x.reshape(t, d).astype(F32)

    w_kv_all = jnp.concatenate([mem_w_kv[l] for l in range(depth)], axis=1).astype(BF16)
    (kv_mem,) = norm_matmul(mem.reshape(bsz * mem_len, d).astype(F32), mem_norm, w_kv_all,
                            [(0, w_kv_all.shape[1], False)], [F32], min(256, bsz * mem_len), "mem_kv")

    o0, o1 = 3 * GDN_WIDTH, 4 * GDN_WIDTH
    kp = vt = cfm = None
    wgu1, wd1 = ffn1_w_gate_up.astype(BF16), ffn1_w_down.astype(BF16)
    wgu2, wd2 = ffn2_w_gate_up.astype(BF16), ffn2_w_down.astype(BF16)
    wo_all = w_out.astype(BF16)
    for l in range(depth):
        last = l == depth - 1
        h = ffn(h, ffn1_norm[l], wgu1, wd1, l, final_norm, tm_ffn, tf, False, "ffn1")
        kbd, vbd = _memory_block_diag(kv_mem[:, 2 * MEM_WIDTH * l:2 * MEM_WIDTH * (l + 1)], bsz, mem_len)
        if l < n_a:
            qkv, z, ab, qmem = gdn_in(h, mix_norm[l], gdn_w_in.astype(F32), l, tm)
            alog_row = _lane_row(gdn_A_log[l])
            dtb_row = _lane_row(gdn_dt_bias[l])
            main = gdn_core(qkv, ab, z, gdn_conv[l].astype(F32), alog_row, dtb_row,
                            gdn_out_norm[l].astype(F32), seq, tl)
        else:
            qp, qmem, bnd = fox_in(h, mix_norm[l], fox_w_in[l - n_a].astype(BF16), cfm, kmax, bsz, tq)
            main = fox_attention(qp, kp, vt, bnd, cfm, bsz, tq)
        h = mix_out(h, main, qmem, kbd, vbd, wo_all, l, bsz, tm)
        h = ffn(h, ffn2_norm[l], wgu2, wd2, l, final_norm, tm_ffn, tf, last, "ffn2")
        if l == n_a - 1:
            w_cat = jnp.concatenate([kv_w[:, :2 * FOX_WIDTH], _pad_cols(kv_w[:, 2 * FOX_WIDTH:], LANES)],
                                    axis=1).astype(BF16)
            vt, k_sh, flog = norm_matmul(
                h, kv_norm, w_cat,
                [(FOX_WIDTH, 2 * FOX_WIDTH, True), (0, FOX_WIDTH, False),
                 (2 * FOX_WIDTH, 2 * FOX_WIDTH + LANES, False)],
                [BF16, BF16, F32], tm, "shared_kv")
            cfm, kp, kn2 = forget_cumsum(flog, _lane_row(kv_b_f), k_sh, bsz, tq)
            kmax = _prefix_max_key_norm(kn2)
    return h.reshape(bsz, seq, d)
```

```python
import functools

import jax
import jax.numpy as jnp
from jax import lax
from jax.experimental import pallas as pl
from jax.experimental.pallas import tpu as pltpu

F32 = jnp.float32
BF16 = jnp.bfloat16

EPS = 1e-6
NEG_INF = -1e30
HEAD_DIM = 128
GDN_HEADS = 6
GDN_WIDTH = GDN_HEADS * HEAD_DIM
FOX_HEADS = 6
FOX_WIDTH = FOX_HEADS * HEAD_DIM
CONV_WIDTH = 4
MEM_HEADS = 4
MEM_HEAD_DIM = 64
MEM_WIDTH = MEM_HEADS * MEM_HEAD_DIM

LANES = 128
SUBLANES = 8
GDN_CHUNK = 128
FOX_AUG = 16
LOG2E = 1.4426950408889634
FOX_SKIP_LOG2 = 160.0
FOX_WINDOW = 3
VMEM_LIMIT = 56 * 1024 * 1024


def _params(*sem):
    return pltpu.CompilerParams(dimension_semantics=sem, vmem_limit_bytes=VMEM_LIMIT)


def _dot(a, b):
    return jnp.dot(a, b, preferred_element_type=F32)


def _dot_nt(a, b):
    return lax.dot_general(a, b, (((1,), (1,)), ((), ())), preferred_element_type=F32)


def _rms(x, gain):
    return x * lax.rsqrt(jnp.mean(x * x, axis=-1, keepdims=True) + EPS) * gain


def _split3(x):
    hi = x.astype(BF16)
    r1 = x - hi.astype(F32)
    mid = r1.astype(BF16)
    lo = (r1 - mid.astype(F32)).astype(BF16)
    return hi, mid, lo


def _tri_cumsum(tri, x):
    hi, mid, lo = _split3(x)
    return _dot(tri, hi) + _dot(tri, mid) + _dot(tri, lo)


def _norm_matmul_kernel(x_ref, g_ref, w_ref, *out_refs, splits):
    xn = _rms(x_ref[...], g_ref[...]).astype(BF16)
    ys = [_dot(xn, w_ref[:, lo:hi]) for lo, hi, _ in splits]
    for (lo, hi, transposed), o_ref, y in zip(splits, out_refs, ys):
        if transposed:
            o_ref[...] = y.T.astype(o_ref.dtype)
        else:
            o_ref[...] = y.astype(o_ref.dtype)


def norm_matmul(x, gain, w, splits, dtypes, tm, name):
    t, d = x.shape
    n = w.shape[1]
    out_shape, out_specs = [], []
    for (lo, hi, transposed), dt in zip(splits, dtypes):
        if transposed:
            out_shape.append(jax.ShapeDtypeStruct((hi - lo, t), dt))
            out_specs.append(pl.BlockSpec((hi - lo, tm), lambda i: (0, i)))
        else:
            out_shape.append(jax.ShapeDtypeStruct((t, hi - lo), dt))
            out_specs.append(pl.BlockSpec((tm, hi - lo), lambda i: (i, 0)))
    return pl.pallas_call(
        functools.partial(_norm_matmul_kernel, splits=tuple(splits)),
        grid=(t // tm,),
        in_specs=[pl.BlockSpec((tm, d), lambda i: (i, 0)),
                  pl.BlockSpec((1, d), lambda i: (0, 0)),
                  pl.BlockSpec((d, n), lambda i: (0, 0))],
        out_specs=out_specs,
        out_shape=out_shape,
        compiler_params=_params("parallel"),
        name=name,
    )(x, gain.reshape(1, d), w)


def _ffn_kernel(h_ref, g_ref, wgu_ref, wd_ref, fg_ref, o_ref, act_ref, *, hidden, tf, final_norm):
    x = h_ref[...]
    xn = _rms(x, g_ref[...]).astype(BF16)
    for j in range(hidden // tf):
        gate = _dot(xn, wgu_ref[:, j * tf:(j + 1) * tf])
        up = _dot(xn, wgu_ref[:, hidden + j * tf:hidden + (j + 1) * tf])
        act_ref[:, j * tf:(j + 1) * tf] = (gate * jax.nn.sigmoid(gate) * up).astype(BF16)
    out = x + 0.5 * _dot(act_ref[...], wd_ref[...])
    if final_norm:
        out = _rms(out, fg_ref[...])
    o_ref[...] = out


def ffn(h, gain, wgu, wd, layer, final_gain, tm, tf, final_norm, name):
    t, d = h.shape
    hidden = wd.shape[1]
    return pl.pallas_call(
        functools.partial(_ffn_kernel, hidden=hidden, tf=tf, final_norm=final_norm),
        grid=(t // tm,),
        in_specs=[pl.BlockSpec((tm, d), lambda i: (i, 0)),
                  pl.BlockSpec((1, d), lambda i: (0, 0)),
                  pl.BlockSpec((None, d, 2 * hidden), lambda i: (layer, 0, 0), pipeline_mode=pl.Buffered(1)),
                  pl.BlockSpec((None, hidden, d), lambda i: (layer, 0, 0), pipeline_mode=pl.Buffered(1)),
                  pl.BlockSpec((1, d), lambda i: (0, 0))],
        out_specs=pl.BlockSpec((tm, d), lambda i: (i, 0)),
        out_shape=jax.ShapeDtypeStruct((t, d), F32),
        scratch_shapes=[pltpu.VMEM((tm, hidden), BF16)],
        compiler_params=_params("parallel"),
        name=name,
    )(h, gain.reshape(1, d), wgu, wd, final_gain.reshape(1, d))


def _unit_lower_inverses(mats):
    n = mats[0].shape[0]
    row = lax.broadcasted_iota(jnp.int32, (n, n), 0)
    col = lax.broadcasted_iota(jnp.int32, (n, n), 1)
    eye = (row == col).astype(F32)
    pair = (row >> 1) == (col >> 1)
    ts = [eye - jnp.where(pair, a, 0.0) for a in mats]
    s = 2
    while s < n:
        k = s.bit_length() - 1
        sub = ((row >> (k + 1)) == (col >> (k + 1))) & ((row >> k) != (col >> k))
        tbs = [t.astype(BF16) for t in ts]
        tes = [_dot(tb, jnp.where(sub, a, 0.0).astype(BF16)).astype(BF16) for tb, a in zip(tbs, mats)]
        ts = [t - _dot(te, tb) for t, te, tb in zip(ts, tes, tbs)]
        s *= 2
    return ts


def _conv_silu(x, prev, cw):
    ln = x.shape[0]
    xs = jnp.concatenate([prev, x], axis=0)
    y = cw[CONV_WIDTH - 1:CONV_WIDTH, :] * x
    for tap in range(CONV_WIDTH - 1):
        shift = CONV_WIDTH - 1 - tap
        y = y + cw[tap:tap + 1, :] * xs[SUBLANES - shift:SUBLANES - shift + ln, :]
    return y * jax.nn.sigmoid(y)


def _gdn_in_kernel(x_ref, g_ref, w_ref, qkv_ref, z_ref, ab_ref, qm_ref, wb_ref):
    @pl.when(pl.program_id(0) == 0)
    def _():
        wb_ref[...] = w_ref[...].astype(BF16)

    o0, o1 = 3 * GDN_WIDTH, 4 * GDN_WIDTH
    ab_off = 2 * GDN_HEADS
    xn = _rms(x_ref[...], g_ref[...]).astype(BF16)
    qkv_ref[...] = _dot(xn, wb_ref[:, :o0])
    z_ref[...] = _dot(xn, wb_ref[:, o0:o1])
    rest = _dot(xn, wb_ref[:, o1:])
    ab_ref[...] = rest[:, :LANES]
    qm_ref[...] = rest[:, ab_off:ab_off + MEM_WIDTH].astype(qm_ref.dtype)


def gdn_in(h, gain, w_in, layer, tm):
    t, d = h.shape
    n = w_in.shape[2]
    w3 = 3 * GDN_WIDTH
    tok = lambda i: (i, 0)
    const = lambda i: (0, 0)
    return pl.pallas_call(
        _gdn_in_kernel,
        grid=(t // tm,),
        in_specs=[pl.BlockSpec((tm, d), tok),
                  pl.BlockSpec((1, d), const),
                  pl.BlockSpec((None, d, n), lambda i: (layer, 0, 0), pipeline_mode=pl.Buffered(1))],
        out_specs=[pl.BlockSpec((tm, w3), tok),
                   pl.BlockSpec((tm, GDN_WIDTH), tok),
                   pl.BlockSpec((tm, LANES), tok),
                   pl.BlockSpec((tm, MEM_WIDTH), tok)],
        out_shape=[jax.ShapeDtypeStruct((t, w3), F32),
                   jax.ShapeDtypeStruct((t, GDN_WIDTH), F32),
                   jax.ShapeDtypeStruct((t, LANES), F32),
                   jax.ShapeDtypeStruct((t, MEM_WIDTH), BF16)],
        scratch_shapes=[pltpu.VMEM((d, n), BF16)],
        compiler_params=_params("arbitrary"),
        name="gdn_in",
    )(h, gain.reshape(1, d), w_in)


def _gdn_core_kernel(x_ref, xp_ref, ab_ref, z_ref, cw_ref, alog_ref, dtb_ref, on_ref, o_ref, s_ref,
                     *, blocks_per_seq):
    c = GDN_CHUNK
    nchunk = x_ref.shape[0] // c
    first = (pl.program_id(0) % blocks_per_seq) == 0

    @pl.when(first)
    def _():
        s_ref[...] = jnp.zeros_like(s_ref)

    prev = jnp.where(first, 0.0, xp_ref[...])
    y = _conv_silu(x_ref[...], prev, cw_ref[...])

    ab = ab_ref[...]
    pre = ab + dtb_ref[...]
    softplus = jnp.maximum(pre, 0.0) + jnp.log(1.0 + jnp.exp(-jnp.abs(pre)))
    g_col = -jnp.exp(alog_ref[...]) * softplus
    beta_col = jax.nn.sigmoid(ab)

    row = lax.broadcasted_iota(jnp.int32, (c, c), 0)
    col = lax.broadcasted_iota(jnp.int32, (c, c), 1)
    lower = row >= col
    tri = lower.astype(BF16)

    mats, rhss, egs, qhs, qks, kts = [], [], [], [], [], []
    for ci in range(nchunk):
        r0 = ci * c
        gc_col = _tri_cumsum(tri, g_col[r0:r0 + c, :])
        gc_row = gc_col.T
        egs.append(jnp.exp(jnp.broadcast_to(gc_row[0:2 * SUBLANES, c - 1:c], (2 * SUBLANES, LANES))))
        for h in range(GDN_HEADS):
            q = y[r0:r0 + c, h * HEAD_DIM:(h + 1) * HEAD_DIM]
            k = y[r0:r0 + c, GDN_WIDTH + h * HEAD_DIM:GDN_WIDTH + (h + 1) * HEAD_DIM]
            v = y[r0:r0 + c, 2 * GDN_WIDTH + h * HEAD_DIM:2 * GDN_WIDTH + (h + 1) * HEAD_DIM]
            q = q * lax.rsqrt(jnp.sum(q * q, axis=-1, keepdims=True) + EPS) * (HEAD_DIM ** -0.5)
            k = k * lax.rsqrt(jnp.sum(k * k, axis=-1, keepdims=True) + EPS)
            beta = beta_col[r0:r0 + c, GDN_HEADS + h:GDN_HEADS + h + 1]
            gcl = gc_col[:, h:h + 1]
            gr = gc_row[h:h + 1, :]
            glast = gr[:, c - 1:c]
            decay = jnp.where(lower, jnp.exp(jnp.where(lower, gcl - gr, 0.0)), 0.0)
            kb = k * beta
            kbf = k.astype(BF16)
            mats.append(jnp.where(row > col, _dot_nt(kb.astype(BF16), kbf) * decay, 0.0))
            qk = _dot_nt(q.astype(BF16), kbf) * decay
            egc = jnp.exp(gcl)
            rhss.append(jnp.concatenate([v * beta, kb * egc], axis=1).astype(BF16))
            qhs.append((q * egc).astype(BF16))
            qks.append(qk.astype(BF16))
            kts.append((k * jnp.exp(glast - gcl)).T.astype(BF16))

    gain = on_ref[...]

    def recurrence_stages(ci, sols):
        rows = slice(ci * c, (ci + 1) * c)
        eg = egs[ci]
        probs = range(ci * GDN_HEADS, (ci + 1) * GDN_HEADS)
        live = {}

        def project():
            live["states"] = [s_ref[h] for h in range(GDN_HEADS)]
            live["sbs"] = [s.astype(BF16) for s in live["states"]]
            live["ws"] = [_dot(sol[:, HEAD_DIM:].astype(BF16), sb) for sol, sb in zip(sols, live["sbs"])]

        def residual():
            live["vbs"] = [(sol[:, :HEAD_DIM] - w).astype(BF16) for sol, w in zip(sols, live["ws"])]

        def advance():
            for h, (p, state, vb) in enumerate(zip(probs, live["states"], live["vbs"])):
                s_ref[h] = state * eg[h:h + 1, :] + _dot(kts[p], vb)

        def emit():
            outs = [_dot(qhs[p], sb) + _dot(qks[p], vb) for p, sb, vb in zip(probs, live["sbs"], live["vbs"])]
            for h, out in enumerate(outs):
                cs = slice(h * HEAD_DIM, (h + 1) * HEAD_DIM)
                zz = z_ref[rows, cs]
                o_ref[rows, cs] = (_rms(out, gain) * (zz * jax.nn.sigmoid(zz))).astype(o_ref.dtype)

        return (project, residual, advance), emit

    sols = [_dot(t.astype(BF16), rhs) for t, rhs in zip(_unit_lower_inverses(mats), rhss)]
    for ci in range(nchunk):
        chain, emit = recurrence_stages(ci, sols[ci * GDN_HEADS:(ci + 1) * GDN_HEADS])
        for stage in chain:
            stage()
        emit()


def gdn_core(qkv, ab, z, conv_w, alog_row, dtb_row, out_norm, seq, tl):
    t = qkv.shape[0]
    w3 = qkv.shape[1]
    width = w3 // 3
    nblk = t // tl
    prev_blocks = tl // SUBLANES
    tok = lambda i: (i, 0)
    const = lambda i: (0, 0)
    return pl.pallas_call(
        functools.partial(_gdn_core_kernel, blocks_per_seq=seq // tl),
        grid=(nblk,),
        in_specs=[pl.BlockSpec((tl, w3), tok),
                  pl.BlockSpec((SUBLANES, w3), lambda i: (jnp.maximum(i * prev_blocks - 1, 0), 0)),
                  pl.BlockSpec((tl, LANES), tok),
                  pl.BlockSpec((tl, width), tok),
                  pl.BlockSpec((CONV_WIDTH, w3), const),
                  pl.BlockSpec((1, LANES), const),
                  pl.BlockSpec((1, LANES), const),
                  pl.BlockSpec((1, HEAD_DIM), const)],
        out_specs=pl.BlockSpec((tl, width), tok),
        out_shape=jax.ShapeDtypeStruct((t, width), BF16),
        scratch_shapes=[pltpu.VMEM((GDN_HEADS, HEAD_DIM, HEAD_DIM), F32)],
        compiler_params=_params("arbitrary"),
        name="gdn_core",
    )(qkv, qkv, ab, z, conv_w, alog_row, dtb_row, out_norm.reshape(1, HEAD_DIM))


def _forget_cumsum_kernel(f_ref, b_ref, k_ref, cfm_ref, kp_ref, kn_ref, carry_ref):
    @pl.when(pl.program_id(1) == 0)
    def _():
        carry_ref[...] = jnp.zeros_like(carry_ref)

    n = f_ref.shape[0]
    pre = f_ref[...] + b_ref[...]
    logf = jnp.minimum(pre, 0.0) - jnp.log(1.0 + jnp.exp(-jnp.abs(pre)))
    row = lax.broadcasted_iota(jnp.int32, (n, n), 0)
    col = lax.broadcasted_iota(jnp.int32, (n, n), 1)
    cs = _tri_cumsum((row >= col).astype(BF16), logf) + carry_ref[...]
    carry_ref[...] = cs[n - 1:n, :]
    cfm_ref[0] = cs.T[:SUBLANES, :]
    hi, mid, lo = (p.astype(F32) for p in _split3(-LOG2E * cs))
    lane = lax.broadcasted_iota(jnp.int32, (n, LANES), 1)
    norms = []
    for h in range(FOX_HEADS):
        aug = jnp.where(lane < 3, 1.0,
                        jnp.where(lane == 3, hi[:, h:h + 1],
                                  jnp.where(lane == 4, mid[:, h:h + 1],
                                            jnp.where(lane == 5, lo[:, h:h + 1], 0.0))))
        kh = k_ref[:, h * HEAD_DIM:(h + 1) * HEAD_DIM]
        kp_ref[h, :, :HEAD_DIM] = kh
        kp_ref[h, :, HEAD_DIM:] = aug[:, :FOX_AUG].astype(BF16)
        kf = kh.astype(F32)
        kn2 = jnp.max(jnp.sum(kf * kf, axis=1, keepdims=True), axis=0, keepdims=True)
        norms.append(jnp.broadcast_to(kn2, (1, LANES)))
    kn_ref[0, 0] = jnp.concatenate(norms + [jnp.zeros((SUBLANES - FOX_HEADS, LANES), F32)], axis=0)


def forget_cumsum(flog, bias_row, k, batch, tl):
    t, width = k.shape
    seq = t // batch
    nblk = seq // tl
    tok = lambda b, i: (b * nblk + i, 0)
    return pl.pallas_call(
        _forget_cumsum_kernel,
        grid=(batch, nblk),
        in_specs=[pl.BlockSpec((tl, LANES), tok),
                  pl.BlockSpec((1, LANES), lambda b, i: (0, 0)),
                  pl.BlockSpec((tl, width), tok)],
        out_specs=[pl.BlockSpec((1, SUBLANES, tl), lambda b, i: (b, 0, i)),
                   pl.BlockSpec((FOX_HEADS, tl, HEAD_DIM + FOX_AUG), lambda b, i: (0, b * nblk + i, 0)),
                   pl.BlockSpec((1, 1, SUBLANES, LANES), lambda b, i: (b, i, 0, 0))],
        out_shape=[jax.ShapeDtypeStruct((batch, SUBLANES, seq), F32),
                   jax.ShapeDtypeStruct((FOX_HEADS, t, HEAD_DIM + FOX_AUG), BF16),
                   jax.ShapeDtypeStruct((batch, nblk, SUBLANES, LANES), F32)],
        scratch_shapes=[pltpu.VMEM((1, LANES), F32)],
        compiler_params=_params("parallel", "arbitrary"),
        name="forget_cumsum",
    )(flog, bias_row, k)


def _fox_in_kernel(x_ref, g_ref, w_ref, cfm_ref, kmax_ref, qp_ref, qm_ref, bnd_ref):
    xn = _rms(x_ref[...], g_ref[...]).astype(BF16)
    q = _dot(xn, w_ref[:, :FOX_WIDTH]) * (LOG2E * HEAD_DIM ** -0.5)
    qm_ref[...] = _dot(xn, w_ref[:, FOX_WIDTH:]).astype(qm_ref.dtype)
    qt = q.T
    tm = qt.shape[1]
    c2 = LOG2E * cfm_ref[0]
    hi, mid, lo = (p.astype(F32) for p in _split3(c2))
    kmax = kmax_ref[0, 0]
    r = lax.broadcasted_iota(jnp.int32, (FOX_AUG, tm), 0)
    bnds = []
    for h in range(FOX_HEADS):
        aug = jnp.where(r == 0, hi[h:h + 1, :],
                        jnp.where(r == 1, mid[h:h + 1, :],
                                  jnp.where(r == 2, lo[h:h + 1, :],
                                            jnp.where(r < 6, 1.0, 0.0))))
        qb = qt[h * HEAD_DIM:(h + 1) * HEAD_DIM, :].astype(BF16)
        qp_ref[h, :HEAD_DIM, :] = qb
        qp_ref[h, HEAD_DIM:, :] = aug.astype(BF16)
        qf = qb.astype(F32)
        q_norm = jnp.sqrt(jnp.sum(qf * qf, axis=0, keepdims=True))
        worst = jnp.max(2.0 * q_norm * kmax[h:h + 1, :1] + c2[h:h + 1, :], axis=1, keepdims=True)
        bnds.append(jnp.broadcast_to(worst + FOX_SKIP_LOG2, (1, LANES)))
    bnd_ref[0, 0] = jnp.concatenate(bnds + [jnp.zeros((SUBLANES - FOX_HEADS, LANES), F32)], axis=0)


def fox_in(x, gain, w, cfm, kmax, batch, tm):
    t, d = x.shape
    per_batch = t // batch // tm
    assert kmax.shape[1] == per_batch
    blk = lambda i: (i // per_batch, i % per_batch, 0, 0)
    return pl.pallas_call(
        _fox_in_kernel,
        grid=(t // tm,),
        in_specs=[pl.BlockSpec((tm, d), lambda i: (i, 0)),
                  pl.BlockSpec((1, d), lambda i: (0, 0)),
                  pl.BlockSpec(w.shape, lambda i: (0, 0)),
                  pl.BlockSpec((1, SUBLANES, tm), lambda i: (i // per_batch, 0, i % per_batch)),
                  pl.BlockSpec((1, 1, SUBLANES, LANES), blk)],
        out_specs=[pl.BlockSpec((FOX_HEADS, HEAD_DIM + FOX_AUG, tm), lambda i: (0, 0, i)),
                   pl.BlockSpec((tm, MEM_WIDTH), lambda i: (i, 0)),
                   pl.BlockSpec((1, 1, SUBLANES, LANES), blk)],
        out_shape=[jax.ShapeDtypeStruct((FOX_HEADS, HEAD_DIM + FOX_AUG, t), BF16),
                   jax.ShapeDtypeStruct((t, MEM_WIDTH), BF16),
                   jax.ShapeDtypeStruct((batch, per_batch, SUBLANES, LANES), F32)],
        compiler_params=_params("parallel"),
        name="fox_in",
    )(x, gain.reshape(1, d), w, cfm, kmax)


def _fox_kernel(it_ref, jt_ref, kind_ref, first_ref, qp_ref, kp_ref, vt_ref, o_ref, m_ref, acc_ref,
                *, tq, tk, nsteps):
    del it_ref, jt_ref
    step = pl.program_id(0) * nsteps + pl.program_id(1)
    kind = kind_ref[step]

    @pl.when(first_ref[step] == 1)
    def _():
        m_ref[...] = jnp.full_like(m_ref, NEG_INF)
        acc_ref[...] = jnp.zeros_like(acc_ref)

    ones_rows = (lax.broadcasted_iota(jnp.int32, (FOX_AUG, tk), 0) == 0).astype(BF16)

    def heads(masked):
        if masked:
            causal = (lax.broadcasted_iota(jnp.int32, (tk, tq), 0)
                      <= lax.broadcasted_iota(jnp.int32, (tk, tq), 1))
        ahead = 2
        scores = [_dot(kp_ref[h], qp_ref[h]) for h in range(ahead)]
        for h in range(FOX_HEADS):
            s = scores[h]
            if h + ahead < FOX_HEADS:
                scores.append(_dot(kp_ref[h + ahead], qp_ref[h + ahead]))
            if masked:
                s = jnp.where(causal, s, NEG_INF)
            m_prev = m_ref[h:h + 1, :]
            m_next = jnp.maximum(m_prev, jnp.max(s, axis=0, keepdims=True))
            alpha = jnp.exp2(m_prev - m_next)
            p = jnp.exp2(s - m_next).astype(BF16)
            vt = jnp.concatenate([vt_ref[h * HEAD_DIM:(h + 1) * HEAD_DIM, :], ones_rows], axis=0)
            acc_ref[h] = alpha * acc_ref[h] + _dot(vt, p)
            m_ref[h:h + 1, :] = m_next

    @pl.when(kind == 1)
    def _():
        heads(False)

    @pl.when(kind == 2)
    def _():
        heads(True)
        for h in range(FOX_HEADS):
            acc = acc_ref[h]
            out = acc[:HEAD_DIM, :] / acc[HEAD_DIM:HEAD_DIM + 1, :]
            o_ref[:, h * HEAD_DIM:(h + 1) * HEAD_DIM] = out.T.astype(o_ref.dtype)


def _prefix_max_key_norm(kn2):
    nblk = kn2.shape[1]
    blk = jnp.arange(nblk)
    upto = (blk[None, :] <= blk[:, None])[None, :, :, None, None]
    return jnp.sqrt(jnp.max(jnp.where(upto, kn2[:, None], 0.0), axis=2))


def _fox_first_needed(bnd, cfm, nq, tq):
    blk = jnp.arange(nq, dtype=jnp.int32)
    bnd = bnd[:, :, :FOX_HEADS, 0]
    cend = (LOG2E * cfm[:, :FOX_HEADS, tq - 1::tq]).transpose(0, 2, 1)
    negligible = jnp.all(cend[:, None, :, :] >= bnd[:, :, None, :], axis=-1)
    needed = (~negligible | (blk[None, :] >= blk[:, None])[None]).astype(jnp.int32)
    return jnp.sum((jnp.cumsum(needed, axis=2) == 0).astype(jnp.int32), axis=2)


def _fox_compact_tables(j_first, nq, nsteps):
    blk = jnp.arange(nq, dtype=jnp.int32)
    count = blk[None, :] + 1 - j_first
    end = jnp.cumsum(count, axis=1)
    start = end - count
    slot = jnp.arange(nsteps, dtype=jnp.int32)[None, :, None]
    it = jnp.minimum(jnp.sum((slot >= end[:, None, :]).astype(jnp.int32), axis=2), nq - 1)
    mine = (it[:, :, None] == blk[None, None, :]).astype(jnp.int32)
    jt = jnp.minimum(jnp.sum(mine * (j_first - start)[:, None, :], axis=2) + slot[:, :, 0], it)
    valid = (slot[:, :, 0] < end[:, nq - 1:nq]).astype(jnp.int32)
    kind = valid * (1 + (jt == it).astype(jnp.int32))
    first = valid * jnp.sum(mine * (start[:, None, :] == slot).astype(jnp.int32), axis=2)
    return it.reshape(-1), jt.reshape(-1), kind.reshape(-1), first.reshape(-1), nsteps


def fox_attention(qp, kp, vt, bnd, cfm, batch, tq):
    heads, kdim, t = qp.shape
    width = vt.shape[0]
    seq = t // batch
    nq = seq // tq
    tk = tq
    j_first = _fox_first_needed(bnd, cfm, nq, tq)

    def run(tables):
        i_tab, j_tab, kind, first, nsteps = tables
        qblk = lambda b, s, it, jt, kd, fs: b * nq + it[b * nsteps + s]
        kblk = lambda b, s, it, jt, kd, fs: b * nq + jt[b * nsteps + s]
        grid_spec = pltpu.PrefetchScalarGridSpec(
            num_scalar_prefetch=4,
            grid=(batch, nsteps),
            in_specs=[pl.BlockSpec((heads, kdim, tq), lambda *a: (0, 0, qblk(*a))),
                      pl.BlockSpec((heads, tk, kdim), lambda *a: (0, kblk(*a), 0)),
                      pl.BlockSpec((width, tk), lambda *a: (0, kblk(*a)))],
            out_specs=pl.BlockSpec((tq, width), lambda *a: (qblk(*a), 0)),
            scratch_shapes=[pltpu.VMEM((SUBLANES, tq), F32),
                            pltpu.VMEM((heads, kdim, tq), F32)])
        return pl.pallas_call(
            functools.partial(_fox_kernel, tq=tq, tk=tk, nsteps=nsteps),
            grid_spec=grid_spec,
            out_shape=jax.ShapeDtypeStruct((t, width), BF16),
            compiler_params=_params("parallel", "arbitrary"),
            name="fox_attention",
        )(i_tab, j_tab, kind, first, qp, kp, vt)

    all_pairs = nq * (nq + 1) // 2
    short = min(FOX_WINDOW * nq, all_pairs)
    needed = jnp.max(jnp.sum(jnp.arange(nq, dtype=jnp.int32)[None, :] + 1 - j_first, axis=1))
    return lax.cond(needed <= short,
                    lambda: run(_fox_compact_tables(j_first, nq, short)),
                    lambda: run(_fox_compact_tables(j_first, nq, all_pairs)))


def _mix_out_kernel(h_ref, main_ref, qm_ref, kbd_ref, vbd_ref, wo_ref, o_ref, *, main_width, mem_len):
    s = _dot(qm_ref[...], kbd_ref[0]) * (MEM_HEAD_DIM ** -0.5)
    acc = h_ref[...] + _dot(main_ref[...], wo_ref[:main_width, :])
    probs = []
    for g in range(MEM_HEADS):
        sg = s[:, g * mem_len:(g + 1) * mem_len]
        p = jnp.exp(sg - jnp.max(sg, axis=1, keepdims=True))
        probs.append((p / jnp.sum(p, axis=1, keepdims=True)).astype(BF16))
    mem_out = _dot(jnp.concatenate(probs, axis=1), vbd_ref[0])
    o_ref[...] = acc + _dot(mem_out.astype(BF16), wo_ref[main_width:, :])


def mix_out(h, main, qmem, kbd, vbd, wo, layer, batch, tm):
    t, d = h.shape
    main_width = main.shape[1]
    mem_len = kbd.shape[2] // MEM_HEADS
    per_batch = t // batch // tm
    tok = lambda i: (i, 0)
    return pl.pallas_call(
        functools.partial(_mix_out_kernel, main_width=main_width, mem_len=mem_len),
        grid=(t // tm,),
        in_specs=[pl.BlockSpec((tm, d), tok),
                  pl.BlockSpec((tm, main_width), tok),
                  pl.BlockSpec((tm, MEM_WIDTH), tok),
                  pl.BlockSpec((1,) + kbd.shape[1:], lambda i: (i // per_batch, 0, 0)),
                  pl.BlockSpec((1,) + vbd.shape[1:], lambda i: (i // per_batch, 0, 0)),
                  pl.BlockSpec((None,) + wo.shape[1:], lambda i: (layer, 0, 0))],
        out_specs=pl.BlockSpec((tm, d), tok),
        out_shape=jax.ShapeDtypeStruct((t, d), F32),
        compiler_params=_params("parallel"),
        name="mix_out",
    )(h, main, qmem, kbd, vbd, wo)


def _pad_cols(w, n):
    return jnp.pad(w, ((0, 0), (0, n - w.shape[1])))


def _lane_row(vec, offset=0):
    return jnp.zeros((1, LANES), F32).at[0, offset:offset + vec.shape[0]].set(vec.astype(F32))


def _memory_block_diag(kv, batch, mem_len):
    kv = kv.reshape(batch, mem_len, 2 * MEM_WIDTH)
    k, v = kv[..., :MEM_WIDTH], kv[..., MEM_WIDTH:]
    head_of_feat = jnp.arange(MEM_WIDTH) // MEM_HEAD_DIM
    head_of_slot = jnp.arange(MEM_HEADS * mem_len) // mem_len
    mask = head_of_feat[:, None] == head_of_slot[None, :]
    kbd = jnp.where(mask[None], jnp.tile(k.transpose(0, 2, 1), (1, 1, MEM_HEADS)), 0).astype(BF16)
    vbd = jnp.where(mask.T[None], jnp.tile(v, (1, MEM_HEADS, 1)), 0).astype(BF16)
    return kbd, vbd


def kernel(x, mem, ffn1_norm, ffn1_w_gate_up, ffn1_w_down, mix_norm, ffn2_norm, ffn2_w_gate_up,
           ffn2_w_down, gdn_w_in, gdn_conv, gdn_A_log, gdn_dt_bias, gdn_out_norm, fox_w_in, w_out,
           mem_norm, mem_w_kv, kv_norm, kv_w, kv_b_f, final_norm):
    bsz, seq, d = x.shape
    mem_len = mem.shape[1]
    depth = ffn1_norm.shape[0]
    n_a = gdn_w_in.shape[0]
    t = bsz * seq
    tm = min(512, seq)
    tf = 256
    tm_ffn = min(1024, seq)
    tl = min(256, seq)
    tq = min(512, seq)

    h = x.reshape(t, d).astype(F32)

    w_kv_all = jnp.concatenate([mem_w_kv[l] for l in range(depth)], axis=1).astype(BF16)
    (kv_mem,) = norm_matmul(mem.reshape(bsz * mem_len, d).astype(F32), mem_norm, w_kv_all,
                            [(0, w_kv_all.shape[1], False)], [F32], min(256, bsz * mem_len), "mem_kv")

    o0, o1 = 3 * GDN_WIDTH, 4 * GDN_WIDTH
    kp = vt = cfm = None
    wgu1, wd1 = ffn1_w_gate_up.astype(BF16), ffn1_w_down.astype(BF16)
    wgu2, wd2 = ffn2_w_gate_up.astype(BF16), ffn2_w_down.astype(BF16)
    wo_all = w_out.astype(BF16)
    for l in range(depth):
        last = l == depth - 1
        h = ffn(h, ffn1_norm[l], wgu1, wd1, l, final_norm, tm_ffn, tf, False, "ffn1")
        kbd, vbd = _memory_block_diag(kv_mem[:, 2 * MEM_WIDTH * l:2 * MEM_WIDTH * (l + 1)], bsz, mem_len)
        if l < n_a:
            qkv, z, ab, qmem = gdn_in(h, mix_norm[l], gdn_w_in.astype(F32), l, tm)
            alog_row = _lane_row(gdn_A_log[l])
            dtb_row = _lane_row(gdn_dt_bias[l])
            main = gdn_core(qkv, ab, z, gdn_conv[l].astype(F32), alog_row, dtb_row,
                            gdn_out_norm[l].astype(F32), seq, tl)
        else:
            qp, qmem, bnd = fox_in(h, mix_norm[l], fox_w_in[l - n_a].astype(BF16), cfm, kmax, bsz, tq)
            main = fox_attention(qp, kp, vt, bnd, cfm, bsz, tq)
        h = mix_out(h, main, qmem, kbd, vbd, wo_all, l, bsz, tm)
        h = ffn(h, ffn2_norm[l], wgu2, wd2, l, final_norm, tm_ffn, tf, last, "ffn2")
        if l == n_a - 1:
            w_cat = jnp.concatenate([kv_w[:, :2 * FOX_WIDTH], _pad_cols(kv_w[:, 2 * FOX_WIDTH:], LANES)],
                                    axis=1).astype(BF16)
            vt, k_sh, flog = norm_matmul(
                h, kv_norm, w_cat,
                [(FOX_WIDTH, 2 * FOX_WIDTH, True), (0, FOX_WIDTH, False),
                 (2 * FOX_WIDTH, 2 * FOX_WIDTH + LANES, False)],
                [BF16, BF16, F32], tm, "shared_kv")
            cfm, kp, kn2 = forget_cumsum(flog, _lane_row(kv_b_f), k_sh, bsz, tq)
            kmax = _prefix_max_key_norm(kn2)
    return h.reshape(bsz, seq, d)
```
